```python
import jax, jax.numpy as jnp
from jax import lax
import numpy as np

D_MODEL = 1024
BATCH = 2
SEQ = 8192
DEPTH = 2

CHUNK = 64
Q_BLOCK = 128
NORM_EPS = 1e-6

GDN_HEADS = 4
GDN_HEAD_DIM = 128
GDN_WIDTH = GDN_HEADS * GDN_HEAD_DIM
CONV_WIDTH = 4

MLA_HEADS = 8
MLA_NOPE_DIM = 64
MLA_ROPE_DIM = 32
MLA_V_DIM = 64
MLA_Q_RANK = 256
MLA_KV_RANK = 128
MLA_WIDTH = MLA_HEADS * MLA_V_DIM
ROPE_THETA = 10000.0

MIX_WIDTH = GDN_WIDTH + MLA_WIDTH
D_FF = 4 * D_MODEL

IN_SIZES = (GDN_WIDTH, GDN_WIDTH, GDN_WIDTH, GDN_WIDTH, GDN_HEADS, GDN_HEADS,
            MLA_Q_RANK, MLA_KV_RANK, MLA_ROPE_DIM)
IN_WIDTH = 4 * GDN_WIDTH + 2 * GDN_HEADS + MLA_Q_RANK + MLA_KV_RANK + MLA_ROPE_DIM

kernel_name = "hybrid_gdn_mla_sandwich_block"


def rms_norm(x, gain):
    xf = x.astype(jnp.float32)
    y = xf * lax.rsqrt(jnp.mean(xf * xf, axis=-1, keepdims=True) + NORM_EPS)
    return (y * gain.astype(jnp.float32)).astype(x.dtype)


def l2_norm(x):
    xf = x.astype(jnp.float32)
    return xf * lax.rsqrt(jnp.sum(xf * xf, axis=-1, keepdims=True) + NORM_EPS)


def causal_depthwise_conv(x, w):
    k = w.shape[0]
    return lax.conv_general_dilated(
        x, w[:, None, :].astype(x.dtype), window_strides=(1,), padding=[(k - 1, 0)],
        dimension_numbers=("NWC", "WIO", "NWC"), feature_group_count=x.shape[-1])


def rope_angles(positions):
    inv_freq = ROPE_THETA ** (-jnp.arange(0, MLA_ROPE_DIM, 2, dtype=jnp.float32) / MLA_ROPE_DIM)
    ang = positions.astype(jnp.float32)[..., None] * inv_freq
    return jnp.cos(ang), jnp.sin(ang)


def apply_rope(x, cos, sin):
    xf = x.astype(jnp.float32)
    x1, x2 = jnp.split(xf, 2, axis=-1)
    return jnp.concatenate([x1 * cos - x2 * sin, x2 * cos + x1 * sin], axis=-1).astype(x.dtype)


def gated_delta_rule(q, k, v, beta, log_a):
    b, s, h, dk = q.shape
    dv = v.shape[-1]
    n = s // CHUNK

    def chunks(t):
        t = t.reshape((b, n, CHUNK, h) + t.shape[3:])
        return jnp.moveaxis(t, 3, 1)

    q, k, v = chunks(q), chunks(k), chunks(v)
    beta, log_a = chunks(beta), chunks(log_a)
    g = jnp.cumsum(log_a, axis=-1)
    idx = jnp.arange(CHUNK)
    strict = idx[:, None] > idx[None, :]
    causal = idx[:, None] >= idx[None, :]
    gdiff = g[..., :, None] - g[..., None, :]
    decay_strict = jnp.exp(jnp.where(strict, gdiff, -jnp.inf))
    decay_causal = jnp.exp(jnp.where(causal, gdiff, -jnp.inf))

    k_beta = k * beta[..., None]
    a_mat = jnp.einsum("bhnid,bhnjd->bhnij", k_beta, k) * decay_strict
    eye = jnp.eye(CHUNK, dtype=jnp.float32)
    t_inv = lax.linalg.triangular_solve(eye + a_mat, jnp.broadcast_to(eye, a_mat.shape),
                                        left_side=True, lower=True)
    u = t_inv @ (v * beta[..., None])
    w = t_inv @ (k_beta * jnp.exp(g)[..., None])
    p = jnp.einsum("bhnid,bhnjd->bhnij", q, k) * decay_causal
    q_dec = q * jnp.exp(g)[..., None]
    g_last = g[..., -1]
    k_dec = k * jnp.exp(g_last[..., None] - g)[..., None]

    def step(state, xs):
        q_c, k_c, u_c, w_c, p_c, gl = xs
        v_new = u_c - jnp.einsum("bhcd,bhde->bhce", w_c, state)
        o = (jnp.einsum("bhcd,bhde->bhce", q_c, state)
             + jnp.einsum("bhij,bhje->bhie", p_c, v_new))
        state = state * jnp.exp(gl)[..., None, None] + jnp.einsum("bhcd,bhce->bhde", k_c, v_new)
        return state, o

    xs = tuple(jnp.moveaxis(t, 2, 0) for t in (q_dec, k_dec, u, w, p, g_last))
    state0 = jnp.zeros((b, h, dk, dv), jnp.float32)
    _, o = lax.scan(step, state0, xs)
    return jnp.transpose(o, (1, 0, 3, 2, 4)).reshape(b, s, h, dv)


def gated_deltanet_group(q, k, v, gate, a_logit, b_logit, conv_w, a_log, dt_bias, out_norm):
    bsz, s, _ = q.shape
    qkv = jax.nn.silu(causal_depthwise_conv(jnp.concatenate([q, k, v], axis=-1), conv_w))
    q, k, v = jnp.split(qkv, 3, axis=-1)
    q = l2_norm(q.reshape(bsz, s, GDN_HEADS, GDN_HEAD_DIM)) * (GDN_HEAD_DIM ** -0.5)
    k = l2_norm(k.reshape(bsz, s, GDN_HEADS, GDN_HEAD_DIM))
    v = v.reshape(bsz, s, GDN_HEADS, GDN_HEAD_DIM).astype(jnp.float32)
    beta = jax.nn.sigmoid(b_logit.astype(jnp.float32))
    log_a = -jnp.exp(a_log.astype(jnp.float32)) * jax.nn.softplus(
        a_logit.astype(jnp.float32) + dt_bias.astype(jnp.float32))
    o = gated_delta_rule(q, k, v, beta, log_a)
    g = jax.nn.silu(gate.astype(jnp.float32)).reshape(bsz, s, GDN_HEADS, GDN_HEAD_DIM)
    o = rms_norm(o, out_norm) * g
    return o.reshape(bsz, s, GDN_WIDTH).astype(gate.dtype)


def mla_group(c_q, c_kv, k_rope, cos, sin, q_norm, w_q_up, kv_norm, w_kv_up):
    b, s, _ = c_q.shape
    dqk = MLA_NOPE_DIM + MLA_ROPE_DIM
    q = (rms_norm(c_q, q_norm) @ w_q_up).reshape(b, s, MLA_HEADS, dqk)
    q_nope, q_rope = q[..., :MLA_NOPE_DIM], q[..., MLA_NOPE_DIM:]
    kv = (rms_norm(c_kv, kv_norm) @ w_kv_up).reshape(b, s, MLA_HEADS, MLA_NOPE_DIM + MLA_V_DIM)
    k_nope, v = kv[..., :MLA_NOPE_DIM], kv[..., MLA_NOPE_DIM:]
    q_rope = apply_rope(q_rope, cos[:, :, None, :], sin[:, :, None, :])
    k_rope = apply_rope(k_rope, cos, sin)
    q = jnp.concatenate([q_nope, q_rope], axis=-1)
    k = jnp.concatenate(
        [k_nope, jnp.broadcast_to(k_rope[:, :, None, :], (b, s, MLA_HEADS, MLA_ROPE_DIM))], axis=-1)
    scale = dqk ** -0.5
    n_blk = s // Q_BLOCK
    q_blocks = jnp.moveaxis(q.reshape(b, n_blk, Q_BLOCK, MLA_HEADS, dqk), 1, 0)
    key_chunk = jnp.arange(s) // CHUNK

    def attend(xs):
        q_b, blk = xs
        q_chunk = (blk * Q_BLOCK + jnp.arange(Q_BLOCK)) // CHUNK
        scores = jnp.einsum("bqhd,bkhd->bhqk", q_b, k,
                            preferred_element_type=jnp.float32) * scale
        mask = key_chunk[None, :] <= q_chunk[:, None]
        probs = jax.nn.softmax(jnp.where(mask, scores, -jnp.inf), axis=-1).astype(v.dtype)
        return jnp.einsum("bhqk,bkhd->bqhd", probs, v)

    o = lax.map(attend, (q_blocks, jnp.arange(n_blk)))
    return jnp.moveaxis(o, 0, 1).reshape(b, s, MLA_WIDTH)


def setup_inputs(seed: int = 0) -> dict:
    key = jax.random.key(seed)
    ks = jax.random.split(key, 20)
    f32 = jnp.float32

    def normal(k, shape, fan_in):
        return jax.random.normal(k, shape, f32) * (fan_in ** -0.5)

    def gain(k, shape):
        return 1.0 + 0.1 * jax.random.normal(k, shape, f32)

    x = jax.random.normal(ks[0], (BATCH, SEQ, D_MODEL), f32)
    offsets = jax.random.randint(ks[1], (BATCH, 1), 0, 4096)
    positions = (offsets + jnp.arange(SEQ)[None, :]).astype(jnp.int32)
    a_log = jnp.log(jax.random.uniform(ks[2], (DEPTH, GDN_HEADS), f32, 1.0, 16.0))
    dt = jnp.exp(jax.random.uniform(ks[3], (DEPTH, GDN_HEADS), f32,
                                    float(np.log(1e-3)), float(np.log(1e-1))))
    dt_bias = dt + jnp.log(-jnp.expm1(-dt))
    return {
        "x": x,
        "positions": positions,
        "mix_pre_norm": gain(ks[4], (DEPTH, D_MODEL)),
        "w_in": normal(ks[5], (DEPTH, D_MODEL, IN_WIDTH), D_MODEL),
        "conv_w": normal(ks[6], (DEPTH, CONV_WIDTH, 3 * GDN_WIDTH), CONV_WIDTH),
        "a_log": a_log,
        "dt_bias": dt_bias,
        "gdn_out_norm": gain(ks[7], (DEPTH, GDN_HEAD_DIM)),
        "q_norm": gain(ks[8], (DEPTH, MLA_Q_RANK)),
        "w_q_up": normal(ks[9], (DEPTH, MLA_Q_RANK, MLA_HEADS * (MLA_NOPE_DIM + MLA_ROPE_DIM)), MLA_Q_RANK),
        "kv_norm": gain(ks[10], (DEPTH, MLA_KV_RANK)),
        "w_kv_up": normal(ks[11], (DEPTH, MLA_KV_RANK, MLA_HEADS * (MLA_NOPE_DIM + MLA_V_DIM)), MLA_KV_RANK),
        "w_out": normal(ks[12], (DEPTH, MIX_WIDTH, D_MODEL), MIX_WIDTH),
        "mix_post_norm": gain(ks[13], (DEPTH, D_MODEL)),
        "ffn_pre_norm": gain(ks[14], (DEPTH, D_MODEL)),
        "w_up": normal(ks[15], (DEPTH, D_MODEL, D_FF), D_MODEL),
        "w_down": normal(ks[16], (DEPTH, D_FF, D_MODEL), D_FF),
        "ffn_post_norm": gain(ks[17], (DEPTH, D_MODEL)),
    }


def reference(x, positions, mix_pre_norm, w_in, conv_w, a_log, dt_bias, gdn_out_norm,
              q_norm, w_q_up, kv_norm, w_kv_up, w_out, mix_post_norm,
              ffn_pre_norm, w_up, w_down, ffn_post_norm):
    cos, sin = rope_angles(positions)
    offsets = [int(o) for o in np.cumsum(IN_SIZES)[:-1]]
    for l in range(DEPTH):
        h = rms_norm(x, mix_pre_norm[l])
        proj = h @ w_in[l]
        gq, gk, gv, gg, ga, gb, cq, ckv, kr = jnp.split(proj, offsets, axis=-1)
        o_gdn = gated_deltanet_group(gq, gk, gv, gg, ga, gb, conv_w[l], a_log[l],
                                     dt_bias[l], gdn_out_norm[l])
        o_mla = mla_group(cq, ckv, kr, cos, sin, q_norm[l], w_q_up[l], kv_norm[l], w_kv_up[l])
        mixed = jnp.concatenate([o_gdn.astype(x.dtype), o_mla.astype(x.dtype)], axis=-1) @ w_out[l]
        x = x + rms_norm(mixed, mix_post_norm[l])
        h = rms_norm(x, ffn_pre_norm[l])
        f = jnp.square(jax.nn.relu(h @ w_up[l])) @ w_down[l]
        x = x + rms_norm(f, ffn_post_norm[l])
    return x
```

```python
import functools

import jax
import jax.numpy as jnp
from jax import lax
from jax.experimental import pallas as pl
from jax.experimental.pallas import tpu as pltpu

F32 = jnp.float32
BF16 = jnp.bfloat16

D_MODEL = 1024
CHUNK = 64
NORM_EPS = 1e-6
GDN_HEADS = 4
GDN_HEAD_DIM = 128
GDN_WIDTH = GDN_HEADS * GDN_HEAD_DIM
CONV_WIDTH = 4
MLA_HEADS = 8
MLA_NOPE_DIM = 64
MLA_ROPE_DIM = 32
MLA_V_DIM = 64
MLA_Q_RANK = 256
MLA_KV_RANK = 128
MLA_WIDTH = MLA_HEADS * MLA_V_DIM
ROPE_THETA = 10000.0
D_FF = 4 * D_MODEL

LANES = 128
SUBLANES = 8
VMEM_LIMIT_BYTES = 56 * 1024 * 1024

COL_QKV = 0
COL_GATE = 3 * GDN_WIDTH
COL_AB = COL_GATE + GDN_WIDTH
COL_CQ = COL_AB + LANES
COL_CKV = COL_CQ + MLA_Q_RANK
COL_KR = COL_CKV + MLA_KV_RANK
N_PROJ = COL_KR + LANES
HEAD_PAD = LANES
ROPE_LANE0 = MLA_NOPE_DIM
ROPE_LANE1 = MLA_NOPE_DIM + MLA_ROPE_DIM

PROJ_TM = 512
FFN_TM = 512
GDN_TS = 256
PAIR = 2 * CHUNK
ATT_TQ = 256
ATT_TK = 256
ROPE_TM = 2048


def _dot(a, b):
    return jnp.dot(a, b, preferred_element_type=F32)


def _dot_nt(a, b):
    return lax.dot_general(a, b, (((1,), (1,)), ((), ())), preferred_element_type=F32)


def _dot_tn(a, b):
    return lax.dot_general(a, b, (((0,), (0,)), ((), ())), preferred_element_type=F32)


def _rms(x, gain):
    return x * lax.rsqrt(jnp.mean(x * x, axis=-1, keepdims=True) + NORM_EPS) * gain


def _silu(x):
    return x * jax.nn.sigmoid(x)


def _full_spec(shape):
    return pl.BlockSpec(shape, lambda *_: (0,) * len(shape))


def _rope_kernel(pos_ref, freq_ref, cos_ref, sin_ref):
    ang = pos_ref[...].astype(F32) * freq_ref[...]
    cos_ref[...] = jnp.cos(ang)
    sin_ref[...] = jnp.sin(ang)


def _rope_tables(pos_col, freq_row):
    t = pos_col.shape[0]
    tm = min(ROPE_TM, t)
    return pl.pallas_call(
        _rope_kernel,
        grid=(t // tm,),
        in_specs=[pl.BlockSpec((tm, 1), lambda i: (i, 0)), _full_spec((1, LANES))],
        out_specs=[pl.BlockSpec((tm, LANES), lambda i: (i, 0))] * 2,
        out_shape=[jax.ShapeDtypeStruct((t, LANES), F32)] * 2,
        name="rope_tables",
    )(pos_col, freq_row)


def _proj_kernel(x_ref, cos_ref, sin_ref, g_ref, w1_ref, qn_ref, wq_ref, kvn_ref, wk_ref, wv_ref,
                 e_ref, vone_ref, qkv_ref, gate_ref, ab_ref, q_ref, k_ref, v_ref):
    h = _rms(x_ref[...], g_ref[...]).astype(BF16)
    qkv_ref[...] = _dot(h, w1_ref[:, COL_QKV:COL_GATE])
    gate_ref[...] = _dot(h, w1_ref[:, COL_GATE:COL_AB])
    ab_ref[...] = _dot(h, w1_ref[:, COL_AB:COL_CQ])
    cq = _dot(h, w1_ref[:, COL_CQ:COL_CKV])
    ckv = _dot(h, w1_ref[:, COL_CKV:COL_KR])
    ykr = _dot(h, w1_ref[:, COL_KR:N_PROJ])

    cos = cos_ref[...]
    sin = sin_ref[...]
    lane = lax.broadcasted_iota(jnp.int32, (1, LANES), 1)
    in_rope = (lane >= ROPE_LANE0) & (lane < ROPE_LANE1)
    scale = (MLA_NOPE_DIM + MLA_ROPE_DIM) ** -0.5 * 1.4426950408889634
    cq_pat = jnp.where(lane < ROPE_LANE1, cos, 0.0) * scale
    sq_pat = jnp.where(in_rope, sin, 0.0) * scale
    n_rep = MLA_HEADS
    yq = _dot(_rms(cq, qn_ref[...]).astype(BF16), wq_ref[...])
    half = MLA_HEADS * HEAD_PAD
    q = yq[:, :half] * jnp.tile(cq_pat, (1, n_rep)) + yq[:, half:] * jnp.tile(sq_pat, (1, n_rep))
    q_ref[...] = q.astype(BF16)

    ckvn = _rms(ckv, kvn_ref[...]).astype(BF16)
    krf = jnp.where(lane >= ROPE_LANE0, ykr * jnp.where(lane < ROPE_LANE1, cos, sin), 0.0)
    k_ref[...] = (_dot(ckvn, wk_ref[...]) + _dot(krf.astype(BF16), e_ref[...])).astype(BF16)
    v_ref[...] = (_dot(ckvn, wv_ref[...]) + vone_ref[...]).astype(BF16)


def _proj_call(x2, cos_t, sin_t, g, w1, qn, wq, kvn, wk, wv, e_mat, vone):
    t = x2.shape[0]
    tm = min(PROJ_TM, t)
    row = lambda w: pl.BlockSpec((tm, w), lambda i: (i, 0))
    hp = MLA_HEADS * HEAD_PAD
    return pl.pallas_call(
        _proj_kernel,
        grid=(t // tm,),
        in_specs=[row(D_MODEL), row(LANES), row(LANES), _full_spec(g.shape), _full_spec(w1.shape),
                  _full_spec(qn.shape), _full_spec(wq.shape), _full_spec(kvn.shape),
                  _full_spec(wk.shape), _full_spec(wv.shape), _full_spec(e_mat.shape),
                  _full_spec(vone.shape)],
        out_specs=[row(3 * GDN_WIDTH), row(GDN_WIDTH), row(LANES), row(hp), row(hp), row(hp)],
        out_shape=[jax.ShapeDtypeStruct((t, 3 * GDN_WIDTH), F32),
                   jax.ShapeDtypeStruct((t, GDN_WIDTH), F32),
                   jax.ShapeDtypeStruct((t, LANES), F32),
                   jax.ShapeDtypeStruct((t, hp), BF16),
                   jax.ShapeDtypeStruct((t, hp), BF16),
                   jax.ShapeDtypeStruct((t, hp), BF16)],
        compiler_params=pltpu.CompilerParams(dimension_semantics=("arbitrary",),
                                             vmem_limit_bytes=VMEM_LIMIT_BYTES),
        name="in_proj",
    )(x2, cos_t, sin_t, g, w1, qn, wq, kvn, wk, wv, e_mat, vone)


def _bmm(a, b):
    return _dot(a.astype(BF16), b.astype(BF16))


def _unit_lower_inverse_minus_eye(a, m16, m32, m64):
    a0 = jnp.where(m16, a, 0.0)
    a2 = _bmm(a0, a0)
    a4 = _bmm(a2, a2)
    a8 = _bmm(a4, a4)
    xs = -a0
    for p in (a2, a4, a8):
        xs = xs + p + _bmm(xs, p)
    for m in (m32, m64):
        off = jnp.where(m, a, 0.0)
        y = off + _bmm(xs, off)
        xs = xs - (y + _bmm(y, xs))
    return xs


def _gdn_kernel(qkv_ref, gate_ref, ab_ref, convw_ref, alog_ref, dtb_ref, onorm_ref, bd_ref, ltri_ref,
                o_ref, xe_s, state_s, g_s, u_s, w_s, qd_s, kd_s, p_s, o_s):
    ts = qkv_ref.shape[0]
    n_pair = ts // PAIR
    hd = GDN_HEAD_DIM

    @pl.when(pl.program_id(1) == 0)
    def _():
        xe_s[0:SUBLANES, :] = jnp.zeros((SUBLANES, 3 * GDN_WIDTH), F32)
        state_s[...] = jnp.zeros_like(state_s)

    xe_s[SUBLANES:, :] = qkv_ref[...]
    y = convw_ref[CONV_WIDTH - 1:CONV_WIDTH, :] * xe_s[SUBLANES:SUBLANES + ts, :]
    for i in range(1, CONV_WIDTH):
        y = y + (convw_ref[CONV_WIDTH - 1 - i:CONV_WIDTH - i, :]
                 * xe_s[SUBLANES - i:SUBLANES - i + ts, :])
    xe_s[0:SUBLANES, :] = qkv_ref[ts - SUBLANES:ts, :]
    y = _silu(y)
    q = y[:, 0:GDN_WIDTH]
    k = y[:, GDN_WIDTH:2 * GDN_WIDTH]
    v = y[:, 2 * GDN_WIDTH:3 * GDN_WIDTH]
    bd = bd_ref[...]
    qn = q * lax.rsqrt(_dot((q * q).astype(BF16), bd) + NORM_EPS) * (hd ** -0.5)
    kn = k * lax.rsqrt(_dot((k * k).astype(BF16), bd) + NORM_EPS)

    ab = ab_ref[...]
    beta = jax.nn.sigmoid(ab)
    z = ab + dtb_ref[...]
    softplus = jnp.maximum(z, 0.0) + jnp.log(1.0 + jnp.exp(-jnp.abs(z)))
    la = -jnp.exp(alog_ref[...]) * softplus
    la_hi = la.astype(BF16)
    r1 = la - la_hi.astype(F32)
    la_mid = r1.astype(BF16)
    la_lo = (r1 - la_mid.astype(F32)).astype(BF16)
    ltri = ltri_ref[...]
    g = _dot(ltri, la_hi) + _dot(ltri, la_mid) + _dot(ltri, la_lo)
    g_s[...] = g
    gt = g.T

    ri = lax.broadcasted_iota(jnp.int32, (PAIR, PAIR), 0)
    ci = lax.broadcasted_iota(jnp.int32, (PAIR, PAIR), 1)
    same = (ri // CHUNK) == (ci // CHUNK)
    causal = same & (ri >= ci)
    strict = same & (ri > ci)
    m16 = strict & ((ri // 16) == (ci // 16))
    m32 = strict & ((ri // 32) == (ci // 32)) & ((ri // 16) != (ci // 16))
    m64 = strict & ((ri // 32) != (ci // 32))
    first = lax.broadcasted_iota(jnp.int32, (PAIR, 1), 0) < CHUNK

    for j in range(n_pair):
        rows = slice(j * PAIR, (j + 1) * PAIR)
        for h in range(GDN_HEADS):
            cols = slice(h * hd, (h + 1) * hd)
            qh, kh, vh = qn[rows, cols], kn[rows, cols], v[rows, cols]
            beta_c = beta[rows, GDN_HEADS + h:GDN_HEADS + h + 1]
            g_c = g[rows, h:h + 1]
            g_r = gt[h:h + 1, rows]
            dc = jnp.exp(jnp.where(causal, g_c - g_r, -jnp.inf))
            ds = jnp.where(strict, dc, 0.0)
            kb = kh * beta_c
            khb = kh.astype(BF16)
            a = _dot_nt(kb.astype(BF16), khb) * ds
            p = _dot_nt(qh.astype(BF16), khb) * dc
            xs = _unit_lower_inverse_minus_eye(a, m16, m32, m64)
            eg = jnp.exp(g_c)
            rhs = jnp.concatenate([vh * beta_c, kb * eg], axis=1)
            uw = rhs + _bmm(xs, rhs)
            g_last = jnp.where(first, g[j * PAIR + CHUNK - 1:j * PAIR + CHUNK, h:h + 1],
                               g[(j + 1) * PAIR - 1:(j + 1) * PAIR, h:h + 1])
            u_s[rows, cols] = uw[:, :hd]
            w_s[rows, cols] = uw[:, hd:].astype(BF16)
            qd_s[rows, cols] = (qh * eg).astype(BF16)
            kd_s[rows, cols] = (kh * jnp.exp(g_last - g_c)).astype(BF16)
            p_s[h, rows, :] = p.astype(BF16)

    def pair_body(j, carry):
        for half in range(2):
            r0 = pl.multiple_of(j * PAIR + half * CHUNK, CHUNK)
            rr = pl.ds(r0, CHUNK)
            gl_row = g_s[pl.ds(r0 + CHUNK - 1, 1), :]
            for h in range(GDN_HEADS):
                cols = slice(h * hd, (h + 1) * hd)
                st = state_s[h]
                wq = jnp.concatenate([w_s[rr, cols], qd_s[rr, cols]], axis=0)
                r = _dot(wq, st.astype(BF16))
                v_new = u_s[rr, cols] - r[0:CHUNK]
                vb = v_new.astype(BF16)
                p_c = p_s[h, rr, half * CHUNK:(half + 1) * CHUNK]
                o_s[rr, cols] = r[CHUNK:2 * CHUNK] + _dot(p_c, vb)
                dec = jnp.exp(gl_row[:, h:h + 1])
                state_s[h] = st * dec + _dot_tn(kd_s[rr, cols], vb)
        return carry

    lax.fori_loop(0, n_pair, pair_body, 0)

    o = o_s[...]
    ms = _dot((o * o).astype(BF16), bd) * (1.0 / hd)
    o_ref[...] = (o * lax.rsqrt(ms + NORM_EPS) * onorm_ref[...] * _silu(gate_ref[...])).astype(BF16)


def _gdn_call(qkv, gate, ab, convw, alog, dtb, onorm, bd, ltri, batch):
    t = qkv.shape[0]
    s = t // batch
    ts = min(GDN_TS, s)
    nsb = s // ts
    row = lambda w: pl.BlockSpec((ts, w), lambda b, i: (b * nsb + i, 0))
    gw = GDN_WIDTH
    return pl.pallas_call(
        _gdn_kernel,
        grid=(batch, nsb),
        in_specs=[row(3 * gw), row(gw), row(LANES), _full_spec(convw.shape), _full_spec(alog.shape),
                  _full_spec(dtb.shape), _full_spec(onorm.shape), _full_spec(bd.shape),
                  _full_spec(ltri.shape)],
        out_specs=row(gw),
        out_shape=jax.ShapeDtypeStruct((t, gw), BF16),
        scratch_shapes=[pltpu.VMEM((ts + SUBLANES, 3 * gw), F32),
                        pltpu.VMEM((GDN_HEADS, GDN_HEAD_DIM, GDN_HEAD_DIM), F32),
                        pltpu.VMEM((ts, LANES), F32),
                        pltpu.VMEM((ts, gw), F32),
                        pltpu.VMEM((ts, gw), BF16),
                        pltpu.VMEM((ts, gw), BF16),
                        pltpu.VMEM((ts, gw), BF16),
                        pltpu.VMEM((GDN_HEADS, ts, PAIR), BF16),
                        pltpu.VMEM((ts, gw), F32)],
        compiler_params=pltpu.CompilerParams(dimension_semantics=("arbitrary", "arbitrary"),
                                             vmem_limit_bytes=VMEM_LIMIT_BYTES),
        name="gated_deltanet",
    )(qkv, gate, ab, convw, alog, dtb, onorm, bd, ltri)


def _att_kernel(q_ref, k_ref, v_ref, o_ref, va_s, vb_s, m_s, acc_s):
    tq = q_ref.shape[0]
    tk = ATT_TK
    qi = pl.program_id(2)
    hp = HEAD_PAD

    @pl.when(qi == 0)
    def _():
        lane = lax.broadcasted_iota(jnp.int32, (1, 2 * hp), 1)
        v = v_ref[...]
        va_s[...] = jnp.where(lane < hp, v, jnp.zeros_like(v))
        vb_s[...] = jnp.where(lane >= hp, v, jnp.zeros_like(v))

    qa = q_ref[:, 0:hp]
    qb = q_ref[:, hp:2 * hp]
    lane2 = lax.broadcasted_iota(jnp.int32, (1, 2 * hp), 1)

    def scores(j0):
        kk = pl.ds(j0, tk)
        return _dot_nt(qa, k_ref[kk, 0:hp]), _dot_nt(qb, k_ref[kk, hp:2 * hp])

    def pv(pa, pb, j0):
        kk = pl.ds(j0, tk)
        return _dot(pa.astype(BF16), va_s[kk, :]) + _dot(pb.astype(BF16), vb_s[kk, :])

    d0 = pl.multiple_of(qi * tq, tq)
    ri = lax.broadcasted_iota(jnp.int32, (tq, tk), 0)
    ci = lax.broadcasted_iota(jnp.int32, (tq, tk), 1)
    visible = (ci // CHUNK) <= (ri // CHUNK)
    sa, sb = scores(d0)
    sa = jnp.where(visible, sa, -jnp.inf)
    sb = jnp.where(visible, sb, -jnp.inf)
    ma = jnp.max(sa, axis=1, keepdims=True)
    mb = jnp.max(sb, axis=1, keepdims=True)
    m_s[0] = ma
    m_s[1] = mb
    acc_s[...] = pv(jnp.exp2(sa - ma), jnp.exp2(sb - mb), d0)

    def body(j, carry):
        j0 = pl.multiple_of(j * tk, tk)
        sa, sb = scores(j0)
        ma_old = m_s[0]
        mb_old = m_s[1]
        ma = jnp.maximum(ma_old, jnp.max(sa, axis=1, keepdims=True))
        mb = jnp.maximum(mb_old, jnp.max(sb, axis=1, keepdims=True))
        alpha = jnp.where(lane2 < hp, jnp.exp2(ma_old - ma), jnp.exp2(mb_old - mb))
        m_s[0] = ma
        m_s[1] = mb
        acc_s[...] = acc_s[...] * alpha + pv(jnp.exp2(sa - ma), jnp.exp2(sb - mb), j0)
        return carry

    lax.fori_loop(0, qi * (tq // tk), body, 0)

    acc = acc_s[...]
    oa = acc[:, 0:hp] / acc[:, MLA_V_DIM:MLA_V_DIM + 1]
    ob = acc[:, hp:2 * hp] / acc[:, hp + MLA_V_DIM:hp + MLA_V_DIM + 1]
    lane = lax.broadcasted_iota(jnp.int32, (1, hp), 1)
    o_ref[...] = jnp.where(lane < MLA_V_DIM, oa, pltpu.roll(ob, MLA_V_DIM, 1)).astype(BF16)


def _att_call(q, k, v, batch):
    t = q.shape[0]
    s = t // batch
    tq = min(ATT_TQ, s)
    nq = s // tq
    n_pairs = MLA_HEADS // 2
    w2 = 2 * HEAD_PAD
    return pl.pallas_call(
        _att_kernel,
        grid=(batch, n_pairs, nq),
        in_specs=[pl.BlockSpec((tq, w2), lambda b, p, i: (b * nq + i, p)),
                  pl.BlockSpec((s, w2), lambda b, p, i: (b, p)),
                  pl.BlockSpec((s, w2), lambda b, p, i: (b, p))],
        out_specs=pl.BlockSpec((tq, LANES), lambda b, p, i: (b * nq + i, p)),
        out_shape=jax.ShapeDtypeStruct((t, MLA_WIDTH), BF16),
        scratch_shapes=[pltpu.VMEM((s, w2), BF16), pltpu.VMEM((s, w2), BF16),
                        pltpu.VMEM((2, tq, 1), F32), pltpu.VMEM((tq, w2), F32)],
        compiler_params=pltpu.CompilerParams(
            dimension_semantics=("arbitrary", "arbitrary", "arbitrary"),
            vmem_limit_bytes=VMEM_LIMIT_BYTES),
        name="mla_attention",
    )(q, k, v)


def _ffn_kernel(x_ref, og_ref, om_ref, wo_ref, g1_ref, g2_ref, wup_ref, wdn_ref, g3_ref, out_ref):
    mixed = _dot(og_ref[...], wo_ref[0:GDN_WIDTH, :]) + _dot(om_ref[...], wo_ref[GDN_WIDTH:, :])
    x1 = x_ref[...] + _rms(mixed, g1_ref[...])
    h = _rms(x1, g2_ref[...]).astype(BF16)
    n_chunk = D_FF // D_MODEL
    f = None
    for c in range(n_chunk):
        cc = slice(c * D_MODEL, (c + 1) * D_MODEL)
        a = jnp.maximum(_dot(h, wup_ref[:, cc]), 0.0)
        part = _dot((a * a).astype(BF16), wdn_ref[cc, :])
        f = part if f is None else f + part
    out_ref[...] = x1 + _rms(f, g3_ref[...])


def _ffn_call(x2, og, om, wo, g1, g2, wup, wdn, g3):
    t = x2.shape[0]
    tm = min(FFN_TM, t)
    row = lambda w: pl.BlockSpec((tm, w), lambda i: (i, 0))
    const = lambda a: pl.BlockSpec(a.shape, lambda i: (0,) * a.ndim, pipeline_mode=pl.Buffered(1))
    return pl.pallas_call(
        _ffn_kernel,
        grid=(t // tm,),
        in_specs=[row(D_MODEL), row(GDN_WIDTH), row(MLA_WIDTH), const(wo), const(g1), const(g2),
                  const(wup), const(wdn), const(g3)],
        out_specs=row(D_MODEL),
        out_shape=jax.ShapeDtypeStruct((t, D_MODEL), F32),
        compiler_params=pltpu.CompilerParams(dimension_semantics=("arbitrary",),
                                             vmem_limit_bytes=VMEM_LIMIT_BYTES),
        name="out_proj_mlp",
    )(x2, og, om, wo, g1, g2, wup, wdn, g3)


def _rot_half(w):
    half = MLA_ROPE_DIM // 2
    return jnp.concatenate([-w[..., half:], w[..., :half]], axis=-1)


def _layer_weights(w_in, w_q_up, w_kv_up):
    d = w_in.shape[0]
    gw = GDN_WIDTH
    o_a = 4 * gw
    o_b = o_a + GDN_HEADS
    o_cq = o_b + GDN_HEADS
    o_ckv = o_cq + MLA_Q_RANK
    o_kr = o_ckv + MLA_KV_RANK
    zeros = lambda n: jnp.zeros((d, n), w_in.dtype)
    kr = w_in[:, o_kr:o_kr + MLA_ROPE_DIM]
    w1 = jnp.concatenate([
        w_in[:, :o_a],
        w_in[:, o_a:o_b], w_in[:, o_b:o_cq], zeros(LANES - 2 * GDN_HEADS),
        w_in[:, o_cq:o_ckv], w_in[:, o_ckv:o_kr],
        zeros(ROPE_LANE0), kr, _rot_half(kr)], axis=1).astype(BF16)

    dqk = MLA_NOPE_DIM + MLA_ROPE_DIM
    wq3 = w_q_up.reshape(MLA_Q_RANK, MLA_HEADS, dqk)
    nope, rope = wq3[..., :MLA_NOPE_DIM], wq3[..., MLA_NOPE_DIM:]
    pad = jnp.zeros((MLA_Q_RANK, MLA_HEADS, HEAD_PAD - dqk), w_q_up.dtype)
    wq_a = jnp.concatenate([nope, rope, pad], axis=-1).reshape(MLA_Q_RANK, MLA_HEADS * HEAD_PAD)
    wq_b = jnp.concatenate([jnp.zeros_like(nope), _rot_half(rope), pad], axis=-1)
    wq = jnp.concatenate([wq_a, wq_b.reshape(MLA_Q_RANK, MLA_HEADS * HEAD_PAD)], axis=1).astype(BF16)

    wkv3 = w_kv_up.reshape(MLA_KV_RANK, MLA_HEADS, MLA_NOPE_DIM + MLA_V_DIM)
    k_nope, v = wkv3[..., :MLA_NOPE_DIM], wkv3[..., MLA_NOPE_DIM:]
    padk = jnp.zeros((MLA_KV_RANK, MLA_HEADS, HEAD_PAD - MLA_NOPE_DIM), w_kv_up.dtype)
    wk = jnp.concatenate([k_nope, padk], axis=-1).reshape(MLA_KV_RANK, MLA_HEADS * HEAD_PAD)
    padv = jnp.zeros((MLA_KV_RANK, MLA_HEADS, HEAD_PAD - MLA_V_DIM), w_kv_up.dtype)
    wv = jnp.concatenate([v, padv], axis=-1).reshape(MLA_KV_RANK, MLA_HEADS * HEAD_PAD)
    return w1, wq, wk.astype(BF16), wv.astype(BF16)


def _constants(ts):
    lane = jnp.arange(LANES)
    src = jnp.where(lane >= ROPE_LANE1, lane - MLA_ROPE_DIM, lane)
    tgt = jnp.arange(MLA_HEADS * HEAD_PAD)
    e_mat = ((src[:, None] == (tgt % HEAD_PAD)[None, :]) & (lane[:, None] >= ROPE_LANE0)).astype(BF16)
    vone = ((tgt % HEAD_PAD) == MLA_V_DIM).astype(F32)[None, :]
    gl = jnp.arange(GDN_WIDTH) // GDN_HEAD_DIM
    bd = (gl[:, None] == gl[None, :]).astype(BF16)
    r = jnp.arange(ts)
    ltri = (((r[:, None] // CHUNK) == (r[None, :] // CHUNK)) & (r[:, None] >= r[None, :])).astype(BF16)
    return e_mat, vone, bd, ltri


def _lane_row(vec, width=LANES):
    return jnp.zeros((1, width), F32).at[0, :vec.shape[0]].set(vec.astype(F32))


def kernel(x, positions, mix_pre_norm, w_in, conv_w, a_log, dt_bias, gdn_out_norm, q_norm, w_q_up,
           kv_norm, w_kv_up, w_out, mix_post_norm, ffn_pre_norm, w_up, w_down, ffn_post_norm):
    batch, seq, d = x.shape
    depth = w_in.shape[0]
    t = batch * seq
    x2 = x.reshape(t, d)
    pos_col = positions.reshape(t, 1)

    inv_freq = ROPE_THETA ** (-jnp.arange(0, MLA_ROPE_DIM, 2, dtype=F32) / MLA_ROPE_DIM)
    lane = jnp.arange(LANES)
    freq_row = jnp.where(lane >= ROPE_LANE0, inv_freq[lane % (MLA_ROPE_DIM // 2)], 0.0)[None, :]
    cos_t, sin_t = _rope_tables(pos_col, freq_row.astype(F32))

    e_mat, vone, bd, ltri = _constants(min(GDN_TS, seq))
    row = lambda v: v.astype(F32)[None, :]
    for l in range(depth):
        w1, wq, wk, wv = _layer_weights(w_in[l], w_q_up[l], w_kv_up[l])
        qkv, gate, ab, q, k, v = _proj_call(
            x2, cos_t, sin_t, row(mix_pre_norm[l]), w1, row(q_norm[l]), wq, row(kv_norm[l]), wk, wv,
            e_mat, vone)
        o_gdn = _gdn_call(qkv, gate, ab, conv_w[l].astype(F32), _lane_row(a_log[l]),
                          _lane_row(dt_bias[l]), row(jnp.tile(gdn_out_norm[l], GDN_HEADS)), bd, ltri,
                          batch)
        o_mla = _att_call(q, k, v, batch)
        x2 = _ffn_call(x2, o_gdn, o_mla, w_out[l].astype(BF16), row(mix_post_norm[l]),
                       row(ffn_pre_norm[l]), w_up[l].astype(BF16), w_down[l].astype(BF16),
                       row(ffn_post_norm[l]))
    return x2.reshape(batch, seq, d)
```

```python
import functools

import jax
import jax.numpy as jnp
from jax import lax
from jax.experimental import pallas as pl
from jax.experimental.pallas import tpu as pltpu

F32 = jnp.float32
BF16 = jnp.bfloat16

D_MODEL = 1024
CHUNK = 64
NORM_EPS = 1e-6
GDN_HEADS = 4
GDN_HEAD_DIM = 128
GDN_WIDTH = GDN_HEADS * GDN_HEAD_DIM
CONV_WIDTH = 4
MLA_HEADS = 8
MLA_NOPE_DIM = 64
MLA_ROPE_DIM = 32
MLA_V_DIM = 64
MLA_Q_RANK = 256
MLA_KV_RANK = 128
MLA_WIDTH = MLA_HEADS * MLA_V_DIM
ROPE_THETA = 10000.0
D_FF = 4 * D_MODEL

LANES = 128
SUBLANES = 8
VMEM_LIMIT_BYTES = 56 * 1024 * 1024

COL_QKV = 0
COL_GATE = 3 * GDN_WIDTH
COL_AB = COL_GATE + GDN_WIDTH
COL_CQ = COL_AB + LANES
COL_CKV = COL_CQ + MLA_Q_RANK
COL_KR = COL_CKV + MLA_KV_RANK
N_PROJ = COL_KR + LANES
HEAD_PAD = LANES
ROPE_LANE0 = MLA_NOPE_DIM
ROPE_LANE1 = MLA_NOPE_DIM + MLA_ROPE_DIM

PROJ_TM = 512
FFN_TM = 512
GDN_TS = 256
PAIR = 2 * CHUNK
ATT_TQ = 512
ROPE_TM = 2048


def _dot(a, b):
    return jnp.dot(a, b, preferred_element_type=F32)


def _dot_nt(a, b):
    return lax.dot_general(a, b, (((1,), (1,)), ((), ())), preferred_element_type=F32)


def _dot_tn(a, b):
    return lax.dot_general(a, b, (((0,), (0,)), ((), ())), preferred_element_type=F32)


def _rms(x, gain):
    return x * lax.rsqrt(jnp.mean(x * x, axis=-1, keepdims=True) + NORM_EPS) * gain


def _silu(x):
    return x * jax.nn.sigmoid(x)


def _full_spec(shape):
    return pl.BlockSpec(shape, lambda *_: (0,) * len(shape))


def _rope_kernel(pos_ref, freq_ref, cos_ref, sin_ref):
    ang = pos_ref[...].astype(F32) * freq_ref[...]
    cos_ref[...] = jnp.cos(ang)
    sin_ref[...] = jnp.sin(ang)


def _rope_tables(pos_col, freq_row):
    t = pos_col.shape[0]
    tm = min(ROPE_TM, t)
    return pl.pallas_call(
        _rope_kernel,
        grid=(t // tm,),
        in_specs=[pl.BlockSpec((tm, 1), lambda i: (i, 0)), _full_spec((1, LANES))],
        out_specs=[pl.BlockSpec((tm, LANES), lambda i: (i, 0))] * 2,
        out_shape=[jax.ShapeDtypeStruct((t, LANES), F32)] * 2,
        name="rope_tables",
    )(pos_col, freq_row)


def _proj_kernel(x_ref, cos_ref, sin_ref, g_ref, w1_ref, qn_ref, wq_ref, kvn_ref, wk_ref, wv_ref,
                 e_ref, vone_ref, qkv_ref, gate_ref, ab_ref, q_ref, k_ref, v_ref):
    h = _rms(x_ref[...], g_ref[...]).astype(BF16)
    qkv_ref[...] = _dot(h, w1_ref[:, COL_QKV:COL_GATE])
    gate_ref[...] = _dot(h, w1_ref[:, COL_GATE:COL_AB])
    ab_ref[...] = _dot(h, w1_ref[:, COL_AB:COL_CQ])
    cq = _dot(h, w1_ref[:, COL_CQ:COL_CKV])
    ckv = _dot(h, w1_ref[:, COL_CKV:COL_KR])
    ykr = _dot(h, w1_ref[:, COL_KR:N_PROJ])

    cos = cos_ref[...]
    sin = sin_ref[...]
    lane = lax.broadcasted_iota(jnp.int32, (1, LANES), 1)
    in_rope = (lane >= ROPE_LANE0) & (lane < ROPE_LANE1)
    scale = (MLA_NOPE_DIM + MLA_ROPE_DIM) ** -0.5 * 1.4426950408889634
    cq_pat = jnp.where(lane < ROPE_LANE1, cos, 0.0) * scale
    sq_pat = jnp.where(in_rope, sin, 0.0) * scale
    n_rep = MLA_HEADS
    yq = _dot(_rms(cq, qn_ref[...]).astype(BF16), wq_ref[...])
    half = MLA_HEADS * HEAD_PAD
    q = yq[:, :half] * jnp.tile(cq_pat, (1, n_rep)) + yq[:, half:] * jnp.tile(sq_pat, (1, n_rep))
    q_ref[...] = q.astype(BF16)

    ckvn = _rms(ckv, kvn_ref[...]).astype(BF16)
    krf = jnp.where(lane >= ROPE_LANE0, ykr * jnp.where(lane < ROPE_LANE1, cos, sin), 0.0)
    k_ref[...] = (_dot(ckvn, wk_ref[...]) + _dot(krf.astype(BF16), e_ref[...])).astype(BF16)
    v_ref[...] = (_dot(ckvn, wv_ref[...]) + vone_ref[...]).astype(BF16)


def _proj_call(x2, cos_t, sin_t, g, w1, qn, wq, kvn, wk, wv, e_mat, vone):
    t = x2.shape[0]
    tm = min(PROJ_TM, t)
    row = lambda w: pl.BlockSpec((tm, w), lambda i: (i, 0))
    hp = MLA_HEADS * HEAD_PAD
    return pl.pallas_call(
        _proj_kernel,
        grid=(t // tm,),
        in_specs=[row(D_MODEL), row(LANES), row(LANES), _full_spec(g.shape), _full_spec(w1.shape),
                  _full_spec(qn.shape), _full_spec(wq.shape), _full_spec(kvn.shape),
                  _full_spec(wk.shape), _full_spec(wv.shape), _full_spec(e_mat.shape),
                  _full_spec(vone.shape)],
        out_specs=[row(3 * GDN_WIDTH), row(GDN_WIDTH), row(LANES), row(hp), row(hp), row(hp)],
        out_shape=[jax.ShapeDtypeStruct((t, 3 * GDN_WIDTH), F32),
                   jax.ShapeDtypeStruct((t, GDN_WIDTH), F32),
                   jax.ShapeDtypeStruct((t, LANES), F32),
                   jax.ShapeDtypeStruct((t, hp), BF16),
                   jax.ShapeDtypeStruct((t, hp), BF16),
                   jax.ShapeDtypeStruct((t, hp), BF16)],
        compiler_params=pltpu.CompilerParams(dimension_semantics=("arbitrary",),
                                             vmem_limit_bytes=VMEM_LIMIT_BYTES),
        name="in_proj",
    )(x2, cos_t, sin_t, g, w1, qn, wq, kvn, wk, wv, e_mat, vone)


def _bmm(a, b):
    return jnp.einsum("uij,ujk->uik", a.astype(BF16), b.astype(BF16), preferred_element_type=F32)


def _unit_lower_inverse_minus_eye(a, m16, m32, m64):
    a0 = jnp.where(m16, a, 0.0)
    a2 = _bmm(a0, a0)
    a4 = _bmm(a2, a2)
    a8 = _bmm(a4, a4)
    xs = -a0
    for p in (a2, a4, a8):
        xs = xs + p + _bmm(xs, p)
    for m in (m32, m64):
        off = jnp.where(m, a, 0.0)
        y = off + _bmm(xs, off)
        xs = xs - (y + _bmm(y, xs))
    return xs


def _gdn_kernel(qkv_ref, gate_ref, ab_ref, convw_ref, alog_ref, dtb_ref, onorm_ref, bd_ref, ltri_ref,
                o_ref, xe_s, state_s):
    nb, ts = qkv_ref.shape[0], qkv_ref.shape[1]
    r = nb * ts
    n_pair = r // PAIR
    pairs_per_batch = ts // PAIR
    hd = GDN_HEAD_DIM
    gw = GDN_WIDTH

    @pl.when(pl.program_id(0) == 0)
    def _():
        xe_s[:, 0:SUBLANES, :] = jnp.zeros((nb, SUBLANES, 3 * gw), F32)
        state_s[...] = jnp.zeros_like(state_s)

    xe_s[:, SUBLANES:, :] = qkv_ref[...]
    y = convw_ref[CONV_WIDTH - 1:CONV_WIDTH, :] * xe_s[:, SUBLANES:SUBLANES + ts, :]
    for i in range(1, CONV_WIDTH):
        y = y + (convw_ref[CONV_WIDTH - 1 - i:CONV_WIDTH - i, :]
                 * xe_s[:, SUBLANES - i:SUBLANES - i + ts, :])
    xe_s[:, 0:SUBLANES, :] = qkv_ref[:, ts - SUBLANES:ts, :]
    y = _silu(y).reshape(r, 3 * gw)
    q = y[:, 0:GDN_WIDTH]
    k = y[:, GDN_WIDTH:2 * GDN_WIDTH]
    v = y[:, 2 * GDN_WIDTH:3 * GDN_WIDTH]
    bd = bd_ref[...]
    qn = q * lax.rsqrt(_dot((q * q).astype(BF16), bd) + NORM_EPS) * (hd ** -0.5)
    kn = k * lax.rsqrt(_dot((k * k).astype(BF16), bd) + NORM_EPS)

    ab = ab_ref[...].reshape(r, LANES)
    beta = jax.nn.sigmoid(ab)
    z = ab + dtb_ref[...]
    softplus = jnp.maximum(z, 0.0) + jnp.log(1.0 + jnp.exp(-jnp.abs(z)))
    la = -jnp.exp(alog_ref[...]) * softplus
    la_hi = la.astype(BF16)
    r1 = la - la_hi.astype(F32)
    la_mid = r1.astype(BF16)
    la_lo = (r1 - la_mid.astype(F32)).astype(BF16)
    ltri = ltri_ref[...]
    g = _dot(ltri, la_hi) + _dot(ltri, la_mid) + _dot(ltri, la_lo)
    gt = g.T

    def per_head_lanes(x, lane0):
        return jnp.concatenate(
            [jnp.broadcast_to(x[:, lane0 + h:lane0 + h + 1], (r, hd)) for h in range(GDN_HEADS)], axis=1)

    g_b = per_head_lanes(g, 0)
    beta_b = per_head_lanes(beta, GDN_HEADS)
    n_chunk = r // CHUNK
    gl_b = jnp.broadcast_to(g_b.reshape(n_chunk, CHUNK, gw)[:, CHUNK - 1:CHUNK, :],
                            (n_chunk, CHUNK, gw)).reshape(r, gw)
    eg = jnp.exp(g_b)
    kb = kn * beta_b
    rhs_v = v * beta_b
    rhs_k = kb * eg
    qd = (qn * eg).astype(BF16)
    kd = kn * jnp.exp(gl_b - g_b)
    dec_b = jnp.exp(gl_b)

    ri = lax.broadcasted_iota(jnp.int32, (PAIR, PAIR), 0)
    ci = lax.broadcasted_iota(jnp.int32, (PAIR, PAIR), 1)
    same = (ri // CHUNK) == (ci // CHUNK)
    causal = same & (ri >= ci)
    strict = same & (ri > ci)
    m16 = strict & ((ri // 16) == (ci // 16))
    m32 = strict & ((ri // 32) == (ci // 32)) & ((ri // 16) != (ci // 16))
    m64 = strict & ((ri // 32) != (ci // 32))
    first_lanes = ci < CHUNK

    units = [(j, h) for j in range(n_pair) for h in range(GDN_HEADS)]
    a_list, p_list, rhs_list = [], [], []
    for j, h in units:
        rows = slice(j * PAIR, (j + 1) * PAIR)
        cols = slice(h * hd, (h + 1) * hd)
        dc = jnp.exp(jnp.where(causal, g_b[rows, cols] - gt[h:h + 1, rows], -jnp.inf))
        khb = kn[rows, cols].astype(BF16)
        a_list.append(_dot_nt(kb[rows, cols].astype(BF16), khb) * jnp.where(strict, dc, 0.0))
        p_list.append((_dot_nt(qn[rows, cols].astype(BF16), khb) * dc).astype(BF16))
        rhs_list.append(jnp.concatenate([rhs_v[rows, cols], rhs_k[rows, cols]], axis=1))
    xs = _unit_lower_inverse_minus_eye(jnp.stack(a_list), m16, m32, m64)
    rhs = jnp.stack(rhs_list)
    uw = rhs + _bmm(xs, rhs)

    nb_mats = {}
    for ui, (j, h) in enumerate(units):
        kdt = kd[j * PAIR:(j + 1) * PAIR, h * hd:(h + 1) * hd].T
        uwb = uw[ui].astype(BF16)
        for half in range(2):
            keep = first_lanes if half == 0 else jnp.logical_not(first_lanes)
            nb_mats[(ui, half)] = _dot(jnp.where(keep, kdt, 0.0).astype(BF16), uwb)

    chunks = [(b, c, h) for c in range(ts // CHUNK) for b in range(nb) for h in range(GDN_HEADS)]
    state = [state_s[i] for i in range(nb * GDN_HEADS)]
    state_in = {}
    for b, c, h in chunks:
        chain = b * GDN_HEADS + h
        ui = (b * pairs_per_batch + c // 2) * GDN_HEADS + h
        r0 = b * ts + c * CHUNK
        st = state[chain]
        sb = st.astype(BF16)
        state_in[(b, c, h)] = sb
        nbm = nb_mats[(ui, c % 2)]
        dec = dec_b[r0:r0 + CHUNK, h * hd:(h + 1) * hd]
        dec = jnp.concatenate([dec, dec], axis=0)
        state[chain] = st * dec - _dot(nbm[:, hd:].astype(BF16), sb) + nbm[:, :hd]
    for i in range(nb * GDN_HEADS):
        state_s[i] = state[i]

    o_rows = []
    zeros_half = jnp.zeros((CHUNK, hd), BF16)
    for b in range(nb):
        for c in range(ts // CHUNK):
            r0 = b * ts + c * CHUNK
            half = c % 2
            o_heads = []
            for h in range(GDN_HEADS):
                ui = (b * pairs_per_batch + c // 2) * GDN_HEADS + h
                cols = slice(h * hd, (h + 1) * hd)
                hr = slice(half * CHUNK, (half + 1) * CHUNK)
                wq = jnp.concatenate([uw[ui, hr, hd:].astype(BF16), qd[r0:r0 + CHUNK, cols]], axis=0)
                rr = _dot(wq, state_in[(b, c, h)])
                v_new = (uw[ui, hr, :hd] - rr[0:CHUNK]).astype(BF16)
                v_pad = jnp.concatenate([v_new, zeros_half] if half == 0 else [zeros_half, v_new], axis=0)
                o_heads.append(rr[CHUNK:2 * CHUNK] + _dot(p_list[ui][hr, :], v_pad))
            o_rows.append(jnp.concatenate(o_heads, axis=1))
    o = jnp.concatenate(o_rows, axis=0)
    ms = _dot((o * o).astype(BF16), bd) * (1.0 / hd)
    gate = gate_ref[...].reshape(r, gw)
    o = o * lax.rsqrt(ms + NORM_EPS) * onorm_ref[...] * _silu(gate)
    o_ref[...] = o.astype(BF16).reshape(nb, ts, gw)


def _gdn_call(qkv, gate, ab, convw, alog, dtb, onorm, bd, ltri, batch):
    t = qkv.shape[0]
    s = t // batch
    ts = min(GDN_TS, s)
    gw = GDN_WIDTH
    blk = lambda w: pl.BlockSpec((batch, ts, w), lambda i: (0, i, 0))
    out = pl.pallas_call(
        _gdn_kernel,
        grid=(s // ts,),
        in_specs=[blk(3 * gw), blk(gw), blk(LANES), _full_spec(convw.shape), _full_spec(alog.shape),
                  _full_spec(dtb.shape), _full_spec(onorm.shape), _full_spec(bd.shape),
                  _full_spec(ltri.shape)],
        out_specs=blk(gw),
        out_shape=jax.ShapeDtypeStruct((batch, s, gw), BF16),
        scratch_shapes=[pltpu.VMEM((batch, ts + SUBLANES, 3 * gw), F32),
                        pltpu.VMEM((batch * GDN_HEADS, GDN_HEAD_DIM, GDN_HEAD_DIM), F32)],
        compiler_params=pltpu.CompilerParams(dimension_semantics=("arbitrary",),
                                             vmem_limit_bytes=VMEM_LIMIT_BYTES),
        name="gated_deltanet",
    )(qkv.reshape(batch, s, 3 * gw), gate.reshape(batch, s, gw), ab.reshape(batch, s, LANES),
      convw, alog, dtb, onorm, bd, ltri)
    return out.reshape(t, gw)


def _att_kernel(q_ref, k_ref, v_ref, o_ref, va_s, vb_s, m_s, acc_s, s_s):
    tq = q_ref.shape[0]
    tk = tq
    qi = pl.program_id(2)
    hp = HEAD_PAD

    @pl.when(qi == 0)
    def _():
        lane = lax.broadcasted_iota(jnp.int32, (1, 2 * hp), 1)
        v = v_ref[...]
        va_s[...] = jnp.where(lane < hp, v, jnp.zeros_like(v))
        vb_s[...] = jnp.where(lane >= hp, v, jnp.zeros_like(v))

    m_s[...] = jnp.full(m_s.shape, -jnp.inf, F32)
    acc_s[...] = jnp.zeros(acc_s.shape, F32)

    def scores(j0, visible):
        kk = pl.ds(j0, tk)
        for hh in range(2):
            hl = slice(hh * hp, (hh + 1) * hp)
            s = _dot_nt(q_ref[:, hl], k_ref[kk, hl])
            s_s[hh] = s if visible is None else jnp.where(visible, s, -jnp.inf)

    def softmax_pv(j0):
        kk = pl.ds(j0, tk)
        alphas, probs = [], []
        for hh in range(2):
            s = s_s[hh]
            m_old = m_s[hh]
            m_new = jnp.maximum(m_old, jnp.max(s, axis=1, keepdims=True))
            m_s[hh] = m_new
            alphas.append(jnp.exp2(m_old - m_new))
            probs.append(jnp.exp2(s - jnp.tile(m_new, (1, tk // LANES))).astype(BF16))
        alpha = jnp.concatenate(alphas, axis=1)
        acc_s[...] = acc_s[...] * alpha + _dot(probs[0], va_s[kk, :]) + _dot(probs[1], vb_s[kk, :])

    ri = lax.broadcasted_iota(jnp.int32, (tq, tk), 0)
    ci = lax.broadcasted_iota(jnp.int32, (tq, tk), 1)
    scores(pl.multiple_of(qi * tq, tq), (ci // CHUNK) <= (ri // CHUNK))

    def body(j, carry):
        cur = jnp.where(j == 0, qi, j - 1)
        softmax_pv(pl.multiple_of(cur * tk, tk))
        scores(pl.multiple_of(j * tk, tk), None)
        return carry

    lax.fori_loop(0, qi, body, 0)
    last = jnp.where(qi == 0, 0, qi - 1)
    softmax_pv(pl.multiple_of(last * tk, tk))

    acc = acc_s[...]
    oa = acc[:, 0:hp] / acc[:, MLA_V_DIM:MLA_V_DIM + 1]
    ob = acc[:, hp:2 * hp] / acc[:, hp + MLA_V_DIM:hp + MLA_V_DIM + 1]
    lane = lax.broadcasted_iota(jnp.int32, (1, hp), 1)
    o_ref[...] = jnp.where(lane < MLA_V_DIM, oa, pltpu.roll(ob, MLA_V_DIM, 1)).astype(BF16)


def _att_call(q, k, v, batch):
    t = q.shape[0]
    s = t // batch
    tq = min(ATT_TQ, s)
    nq = s // tq
    n_pairs = MLA_HEADS // 2
    w2 = 2 * HEAD_PAD
    return pl.pallas_call(
        _att_kernel,
        grid=(batch, n_pairs, nq),
        in_specs=[pl.BlockSpec((tq, w2), lambda b, p, i: (b * nq + i, p)),
                  pl.BlockSpec((s, w2), lambda b, p, i: (b, p)),
                  pl.BlockSpec((s, w2), lambda b, p, i: (b, p))],
        out_specs=pl.BlockSpec((tq, LANES), lambda b, p, i: (b * nq + i, p)),
        out_shape=jax.ShapeDtypeStruct((t, MLA_WIDTH), BF16),
        scratch_shapes=[pltpu.VMEM((s, w2), BF16), pltpu.VMEM((s, w2), BF16),
                        pltpu.VMEM((2, tq, LANES), F32), pltpu.VMEM((tq, w2), F32),
                        pltpu.VMEM((2, tq, tq), F32)],
        compiler_params=pltpu.CompilerParams(
            dimension_semantics=("arbitrary", "arbitrary", "arbitrary"),
            vmem_limit_bytes=VMEM_LIMIT_BYTES),
        name="mla_attention",
    )(q, k, v)


def _ffn_kernel(x_ref, og_ref, om_ref, wo_ref, g1_ref, g2_ref, wup_ref, wdn_ref, g3_ref, out_ref):
    mixed = _dot(og_ref[...], wo_ref[0:GDN_WIDTH, :]) + _dot(om_ref[...], wo_ref[GDN_WIDTH:, :])
    x1 = x_ref[...] + _rms(mixed, g1_ref[...])
    h = _rms(x1, g2_ref[...]).astype(BF16)
    n_chunk = D_FF // D_MODEL
    f = None
    for c in range(n_chunk):
        cc = slice(c * D_MODEL, (c + 1) * D_MODEL)
        a = jnp.maximum(_dot(h, wup_ref[:, cc]), 0.0)
        part = _dot((a * a).astype(BF16), wdn_ref[cc, :])
        f = part if f is None else f + part
    out_ref[...] = x1 + _rms(f, g3_ref[...])


def _ffn_call(x2, og, om, wo, g1, g2, wup, wdn, g3):
    t = x2.shape[0]
    tm = min(FFN_TM, t)
    row = lambda w: pl.BlockSpec((tm, w), lambda i: (i, 0))
    const = lambda a: pl.BlockSpec(a.shape, lambda i: (0,) * a.ndim, pipeline_mode=pl.Buffered(1))
    return pl.pallas_call(
        _ffn_kernel,
        grid=(t // tm,),
        in_specs=[row(D_MODEL), row(GDN_WIDTH), row(MLA_WIDTH), const(wo), const(g1), const(g2),
                  const(wup), const(wdn), const(g3)],
        out_specs=row(D_MODEL),
        out_shape=jax.ShapeDtypeStruct((t, D_MODEL), F32),
        compiler_params=pltpu.CompilerParams(dimension_semantics=("arbitrary",),
                                             vmem_limit_bytes=VMEM_LIMIT_BYTES),
        name="out_proj_mlp",
    )(x2, og, om, wo, g1, g2, wup, wdn, g3)


def _rot_half(w):
    half = MLA_ROPE_DIM // 2
    return jnp.concatenate([-w[..., half:], w[..., :half]], axis=-1)


def _layer_weights(w_in, w_q_up, w_kv_up):
    d = w_in.shape[0]
    gw = GDN_WIDTH
    o_a = 4 * gw
    o_b = o_a + GDN_HEADS
    o_cq = o_b + GDN_HEADS
    o_ckv = o_cq + MLA_Q_RANK
    o_kr = o_ckv + MLA_KV_RANK
    zeros = lambda n: jnp.zeros((d, n), w_in.dtype)
    kr = w_in[:, o_kr:o_kr + MLA_ROPE_DIM]
    w1 = jnp.concatenate([
        w_in[:, :o_a],
        w_in[:, o_a:o_b], w_in[:, o_b:o_cq], zeros(LANES - 2 * GDN_HEADS),
        w_in[:, o_cq:o_ckv], w_in[:, o_ckv:o_kr],
        zeros(ROPE_LANE0), kr, _rot_half(kr)], axis=1).astype(BF16)

    dqk = MLA_NOPE_DIM + MLA_ROPE_DIM
    wq3 = w_q_up.reshape(MLA_Q_RANK, MLA_HEADS, dqk)
    nope, rope = wq3[..., :MLA_NOPE_DIM], wq3[..., MLA_NOPE_DIM:]
    pad = jnp.zeros((MLA_Q_RANK, MLA_HEADS, HEAD_PAD - dqk), w_q_up.dtype)
    wq_a = jnp.concatenate([nope, rope, pad], axis=-1).reshape(MLA_Q_RANK, MLA_HEADS * HEAD_PAD)
    wq_b = jnp.concatenate([jnp.zeros_like(nope), _rot_half(rope), pad], axis=-1)
    wq = jnp.concatenate([wq_a, wq_b.reshape(MLA_Q_RANK, MLA_HEADS * HEAD_PAD)], axis=1).astype(BF16)

    wkv3 = w_kv_up.reshape(MLA_KV_RANK, MLA_HEADS, MLA_NOPE_DIM + MLA_V_DIM)
    k_nope, v = wkv3[..., :MLA_NOPE_DIM], wkv3[..., MLA_NOPE_DIM:]
    padk = jnp.zeros((MLA_KV_RANK, MLA_HEADS, HEAD_PAD - MLA_NOPE_DIM), w_kv_up.dtype)
    wk = jnp.concatenate([k_nope, padk], axis=-1).reshape(MLA_KV_RANK, MLA_HEADS * HEAD_PAD)
    padv = jnp.zeros((MLA_KV_RANK, MLA_HEADS, HEAD_PAD - MLA_V_DIM), w_kv_up.dtype)
    wv = jnp.concatenate([v, padv], axis=-1).reshape(MLA_KV_RANK, MLA_HEADS * HEAD_PAD)
    return w1, wq, wk.astype(BF16), wv.astype(BF16)


def _constants(ts):
    lane = jnp.arange(LANES)
    src = jnp.where(lane >= ROPE_LANE1, lane - MLA_ROPE_DIM, lane)
    tgt = jnp.arange(MLA_HEADS * HEAD_PAD)
    e_mat = ((src[:, None] == (tgt % HEAD_PAD)[None, :]) & (lane[:, None] >= ROPE_LANE0)).astype(BF16)
    vone = ((tgt % HEAD_PAD) == MLA_V_DIM).astype(F32)[None, :]
    gl = jnp.arange(GDN_WIDTH) // GDN_HEAD_DIM
    bd = (gl[:, None] == gl[None, :]).astype(BF16)
    r = jnp.arange(ts)
    ltri = (((r[:, None] // CHUNK) == (r[None, :] // CHUNK)) & (r[:, None] >= r[None, :])).astype(BF16)
    return e_mat, vone, bd, ltri


def _lane_row(vec, width=LANES):
    return jnp.zeros((1, width), F32).at[0, :vec.shape[0]].set(vec.astype(F32))


def kernel(x, positions, mix_pre_norm, w_in, conv_w, a_log, dt_bias, gdn_out_norm, q_norm, w_q_up,
           kv_norm, w_kv_up, w_out, mix_post_norm, ffn_pre_norm, w_up, w_down, ffn_post_norm):
    batch, seq, d = x.shape
    depth = w_in.shape[0]
    t = batch * seq
    x2 = x.reshape(t, d)
    pos_col = positions.reshape(t, 1)

    inv_freq = ROPE_THETA ** (-jnp.arange(0, MLA_ROPE_DIM, 2, dtype=F32) / MLA_ROPE_DIM)
    lane = jnp.arange(LANES)
    freq_row = jnp.where(lane >= ROPE_LANE0, inv_freq[lane % (MLA_ROPE_DIM // 2)], 0.0)[None, :]
    cos_t, sin_t = _rope_tables(pos_col, freq_row.astype(F32))

    e_mat, vone, bd, ltri = _constants(batch * min(GDN_TS, seq))
    row = lambda v: v.astype(F32)[None, :]
    for l in range(depth):
        w1, wq, wk, wv = _layer_weights(w_in[l], w_q_up[l], w_kv_up[l])
        qkv, gate, ab, q, k, v = _proj_call(
            x2, cos_t, sin_t, row(mix_pre_norm[l]), w1, row(q_norm[l]), wq, row(kv_norm[l]), wk, wv,
            e_mat, vone)
        o_gdn = _gdn_call(qkv, gate, ab, conv_w[l].astype(F32), _lane_row(a_log[l]),
                          _lane_row(dt_bias[l]), row(jnp.tile(gdn_out_norm[l], GDN_HEADS)), bd, ltri,
                          batch)
        o_mla = _att_call(q, k, v, batch)
        x2 = _ffn_call(x2, o_gdn, o_mla, w_out[l].astype(BF16), row(mix_post_norm[l]),
                       row(ffn_pre_norm[l]), w_up[l].astype(BF16), w_down[l].astype(BF16),
                       row(ffn_post_norm[l]))
    return x2.reshape(batch, seq, d)
```

```python
import jax
import jax.numpy as jnp
import numpy as np
from jax import lax
from jax.experimental import pallas as pl
from jax.experimental.pallas import tpu as pltpu

F32 = jnp.float32
BF16 = jnp.bfloat16

D_MODEL = 1024
CHUNK = 64
NORM_EPS = 1e-6
GDN_HEADS = 4
GDN_HEAD_DIM = 128
GDN_WIDTH = GDN_HEADS * GDN_HEAD_DIM
CONV_WIDTH = 4
MLA_HEADS = 8
MLA_NOPE_DIM = 64
MLA_ROPE_DIM = 32
MLA_V_DIM = 64
MLA_Q_RANK = 256
MLA_KV_RANK = 128
MLA_WIDTH = MLA_HEADS * MLA_V_DIM
ROPE_THETA = 10000.0
D_FF = 4 * D_MODEL

LANES = 128
SUBLANES = 8
VMEM_LIMIT_BYTES = 56 * 1024 * 1024

COL_QKV = 0
COL_GATE = 3 * GDN_WIDTH
COL_AB = COL_GATE + GDN_WIDTH
COL_CQ = COL_AB + LANES
COL_CKV = COL_CQ + MLA_Q_RANK
COL_KR = COL_CKV + MLA_KV_RANK
N_PROJ = COL_KR + LANES
HEAD_PAD = LANES
ROPE_LANE0 = MLA_NOPE_DIM
ROPE_LANE1 = MLA_NOPE_DIM + MLA_ROPE_DIM

PROJ_TM = 512
FFN_TM = 512
GDN_TS = 256
PAIR = 2 * CHUNK
ATT_TQ = 512
ATT_UNROLL = 4
ROPE_TM = 2048


def _dot(a, b):
    return jnp.dot(a, b, preferred_element_type=F32)


def _dot_nt(a, b):
    return lax.dot_general(a, b, (((1,), (1,)), ((), ())), preferred_element_type=F32)


def _dot_tn(a, b):
    return lax.dot_general(a, b, (((0,), (0,)), ((), ())), preferred_element_type=F32)


def _rms(x, gain):
    return x * lax.rsqrt(jnp.mean(x * x, axis=-1, keepdims=True) + NORM_EPS) * gain


def _silu(x):
    return x * jax.nn.sigmoid(x)


def _full_spec(shape):
    return pl.BlockSpec(shape, lambda *_: (0,) * len(shape))


def _rope_kernel(pos_ref, freq_ref, cos_ref, sin_ref):
    ang = pos_ref[...].astype(F32) * freq_ref[...]
    cos_ref[...] = jnp.cos(ang)
    sin_ref[...] = jnp.sin(ang)


def _rope_tables(pos_col, freq_row):
    t = pos_col.shape[0]
    tm = min(ROPE_TM, t)
    return pl.pallas_call(
        _rope_kernel,
        grid=(t // tm,),
        in_specs=[pl.BlockSpec((tm, 1), lambda i: (i, 0)), _full_spec((1, LANES))],
        out_specs=[pl.BlockSpec((tm, LANES), lambda i: (i, 0))] * 2,
        out_shape=[jax.ShapeDtypeStruct((t, LANES), F32)] * 2,
        name="rope_tables",
    )(pos_col, freq_row)


def _proj_kernel(x_ref, cos_ref, sin_ref, g_ref, w1_ref, qn_ref, wq_ref, kvn_ref, wk_ref, wv_ref,
                 e_ref, vone_ref, qkv_ref, gate_ref, ab_ref, q_ref, k_ref, v_ref):
    h = _rms(x_ref[...], g_ref[...]).astype(BF16)
    qkv_ref[...] = _dot(h, w1_ref[:, COL_QKV:COL_GATE])
    gate_ref[...] = _dot(h, w1_ref[:, COL_GATE:COL_AB])
    ab_ref[...] = _dot(h, w1_ref[:, COL_AB:COL_CQ])
    cq = _dot(h, w1_ref[:, COL_CQ:COL_CKV])
    ckv = _dot(h, w1_ref[:, COL_CKV:COL_KR])
    ykr = _dot(h, w1_ref[:, COL_KR:N_PROJ])

    cos = cos_ref[...]
    sin = sin_ref[...]
    lane = lax.broadcasted_iota(jnp.int32, (1, LANES), 1)
    in_rope = (lane >= ROPE_LANE0) & (lane < ROPE_LANE1)
    scale = (MLA_NOPE_DIM + MLA_ROPE_DIM) ** -0.5 * 1.4426950408889634
    cq_pat = jnp.where(lane < ROPE_LANE1, cos, 0.0) * scale
    sq_pat = jnp.where(in_rope, sin, 0.0) * scale
    n_rep = MLA_HEADS
    yq = _dot(_rms(cq, qn_ref[...]).astype(BF16), wq_ref[...])
    half = MLA_HEADS * HEAD_PAD
    q = yq[:, :half] * jnp.tile(cq_pat, (1, n_rep)) + yq[:, half:] * jnp.tile(sq_pat, (1, n_rep))
    q_ref[...] = q.astype(BF16)

    ckvn = _rms(ckv, kvn_ref[...]).astype(BF16)
    krf = jnp.where(lane >= ROPE_LANE0, ykr * jnp.where(lane < ROPE_LANE1, cos, sin), 0.0)
    k_ref[...] = (_dot(ckvn, wk_ref[...]) + _dot(krf.astype(BF16), e_ref[...])).astype(BF16)
    v_ref[...] = (_dot(ckvn, wv_ref[...]) + vone_ref[...]).astype(BF16)


def _proj_call(x2, cos_t, sin_t, g, w1, qn, wq, kvn, wk, wv, e_mat, vone):
    t = x2.shape[0]
    tm = min(PROJ_TM, t)
    row = lambda w: pl.BlockSpec((tm, w), lambda i: (i, 0))
    hp = MLA_HEADS * HEAD_PAD
    return pl.pallas_call(
        _proj_kernel,
        grid=(t // tm,),
        in_specs=[row(D_MODEL), row(LANES), row(LANES), _full_spec(g.shape), _full_spec(w1.shape),
                  _full_spec(qn.shape), _full_spec(wq.shape), _full_spec(kvn.shape),
                  _full_spec(wk.shape), _full_spec(wv.shape), _full_spec(e_mat.shape),
                  _full_spec(vone.shape)],
        out_specs=[row(3 * GDN_WIDTH), row(GDN_WIDTH), row(LANES), row(hp), row(hp), row(hp)],
        out_shape=[jax.ShapeDtypeStruct((t, 3 * GDN_WIDTH), F32),
                   jax.ShapeDtypeStruct((t, GDN_WIDTH), F32),
                   jax.ShapeDtypeStruct((t, LANES), F32),
                   jax.ShapeDtypeStruct((t, hp), BF16),
                   jax.ShapeDtypeStruct((t, hp), BF16),
                   jax.ShapeDtypeStruct((t, hp), BF16)],
        compiler_params=pltpu.CompilerParams(dimension_semantics=("arbitrary",),
                                             vmem_limit_bytes=VMEM_LIMIT_BYTES),
        name="in_proj",
    )(x2, cos_t, sin_t, g, w1, qn, wq, kvn, wk, wv, e_mat, vone)


def _bmm(a, b):
    return jnp.einsum("uij,ujk->uik", a.astype(BF16), b.astype(BF16), preferred_element_type=F32)


def _bmm_nt(a, b):
    return jnp.einsum("uik,ujk->uij", a.astype(BF16), b.astype(BF16), preferred_element_type=F32)


def _unit_lower_inverse_minus_eye(a, m16, m32, m64):
    a0 = jnp.where(m16, a, 0.0)
    a2 = _bmm(a0, a0)
    a4 = _bmm(a2, a2)
    a8 = _bmm(a4, a4)
    xs = -a0
    for p in (a2, a4, a8):
        xs = xs + p + _bmm(xs, p)
    for m in (m32, m64):
        off = jnp.where(m, a, 0.0)
        y = off + _bmm(xs, off)
        xs = xs - (y + _bmm(y, xs))
    return xs


def _gdn_kernel(qkv_ref, gate_ref, ab_ref, convw_ref, alog_ref, dtb_ref, onorm_ref, bd_ref, ltri_ref,
                o_ref, xe_s, state_s, qn_s, kn_s, v_s, g_s, beta_s):
    nb, ts = qkv_ref.shape[0], qkv_ref.shape[1]
    r = nb * ts
    n_pair = r // PAIR
    pairs_per_batch = ts // PAIR
    hd = GDN_HEAD_DIM
    gw = GDN_WIDTH
    step = pl.program_id(0)

    @pl.when(step == 0)
    def _():
        xe_s[:, 0:SUBLANES, :] = jnp.zeros((nb, SUBLANES, 3 * gw), F32)
        state_s[...] = jnp.zeros_like(state_s)
        for ref in (qn_s, kn_s, v_s, g_s, beta_s):
            ref[...] = jnp.zeros_like(ref)

    wr = jnp.bitwise_and(step, 1)
    rd = 1 - wr
    qn = qn_s[rd]
    kn = kn_s[rd]
    v = v_s[rd]
    g = g_s[rd]
    beta = beta_s[rd]
    bd = bd_ref[...]
    gt = g.T

    def per_head_lanes(x, lane0):
        return jnp.concatenate(
            [jnp.broadcast_to(x[:, lane0 + h:lane0 + h + 1], (r, hd)) for h in range(GDN_HEADS)], axis=1)

    g_b = per_head_lanes(g, 0)
    beta_b = per_head_lanes(beta, GDN_HEADS)
    n_chunk = r // CHUNK
    gl_b = jnp.broadcast_to(g_b.reshape(n_chunk, CHUNK, gw)[:, CHUNK - 1:CHUNK, :],
                            (n_chunk, CHUNK, gw)).reshape(r, gw)
    eg = jnp.exp(g_b)
    kb = kn * beta_b
    rhs_v = v * beta_b
    rhs_k = kb * eg
    qd = (qn * eg).astype(BF16)
    kd = kn * jnp.exp(gl_b - g_b)
    dec_b = jnp.exp(gl_b)

    ri = lax.broadcasted_iota(jnp.int32, (PAIR, PAIR), 0)
    ci = lax.broadcasted_iota(jnp.int32, (PAIR, PAIR), 1)
    same = (ri // CHUNK) == (ci // CHUNK)
    causal = same & (ri >= ci)
    strict = same & (ri > ci)
    m16 = strict & ((ri // 16) == (ci // 16))
    m32 = strict & ((ri // 32) == (ci // 32)) & ((ri // 16) != (ci // 16))
    m64 = strict & ((ri // 32) != (ci // 32))
    first_lanes = ci < CHUNK

    units = [(j, h) for j in range(n_pair) for h in range(GDN_HEADS)]

    def tile(x, j, h):
        return x[j * PAIR:(j + 1) * PAIR, h * hd:(h + 1) * hd]

    def stack(x):
        return jnp.stack([tile(x, j, h) for j, h in units])

    k_u = stack(kn)
    g_rows = jnp.stack([gt[h:h + 1, j * PAIR:(j + 1) * PAIR] for j, h in units])
    dc = jnp.exp(jnp.where(causal, stack(g_b) - g_rows, -jnp.inf))
    a = _bmm_nt(stack(kb), k_u) * jnp.where(strict, dc, 0.0)
    p = (_bmm_nt(stack(qn), k_u) * dc).astype(BF16)
    xs = _unit_lower_inverse_minus_eye(a, m16, m32, m64)
    rhs = jnp.concatenate([stack(rhs_v), stack(rhs_k)], axis=2)
    uw = rhs + _bmm(xs, rhs)

    kdt = jnp.stack([tile(kd, j, h).T for j, h in units])
    uwb = uw.astype(BF16)
    nbm = [_bmm(jnp.where(first_lanes, kdt, 0.0), uwb),
           _bmm(jnp.where(first_lanes, 0.0, kdt), uwb)]

    xe_s[:, SUBLANES:, :] = qkv_ref[...]
    y = convw_ref[CONV_WIDTH - 1:CONV_WIDTH, :] * xe_s[:, SUBLANES:SUBLANES + ts, :]
    for i in range(1, CONV_WIDTH):
        y = y + (convw_ref[CONV_WIDTH - 1 - i:CONV_WIDTH - i, :]
                 * xe_s[:, SUBLANES - i:SUBLANES - i + ts, :])
    xe_s[:, 0:SUBLANES, :] = qkv_ref[:, ts - SUBLANES:ts, :]
    y = _silu(y).reshape(r, 3 * gw)
    q_in = y[:, 0:gw]
    k_in = y[:, gw:2 * gw]
    qn_s[wr] = q_in * lax.rsqrt(_dot((q_in * q_in).astype(BF16), bd) + NORM_EPS) * (hd ** -0.5)
    kn_s[wr] = k_in * lax.rsqrt(_dot((k_in * k_in).astype(BF16), bd) + NORM_EPS)
    v_s[wr] = y[:, 2 * gw:3 * gw]

    ab = ab_ref[...].reshape(r, LANES)
    beta_s[wr] = jax.nn.sigmoid(ab)
    z = ab + dtb_ref[...]
    softplus = jnp.maximum(z, 0.0) + jnp.log(1.0 + jnp.exp(-jnp.abs(z)))
    la = -jnp.exp(alog_ref[...]) * softplus
    la_hi = la.astype(BF16)
    r1 = la - la_hi.astype(F32)
    la_mid = r1.astype(BF16)
    la_lo = (r1 - la_mid.astype(F32)).astype(BF16)
    ltri = ltri_ref[...]
    g_s[wr] = _dot(ltri, la_hi) + _dot(ltri, la_mid) + _dot(ltri, la_lo)

    n_c = ts // CHUNK
    chains = [(b, h) for b in range(nb) for h in range(GDN_HEADS)]

    def unit_of(b, c, h):
        return (b * pairs_per_batch + c // 2) * GDN_HEADS + h

    st = state_s[...]
    state_in = []
    for c in range(n_c):
        nb_c = jnp.stack([nbm[c % 2][unit_of(b, c, h)] for b, h in chains])
        dec = jnp.stack([jnp.tile(dec_b[b * ts + c * CHUNK:b * ts + (c + 1) * CHUNK, h * hd:(h + 1) * hd],
                                  (2, 1)) for b, h in chains])
        sb = st.astype(BF16)
        state_in.append(sb)
        st = st * dec - _bmm(nb_c[:, :, hd:], sb) + nb_c[:, :, :hd]
    state_s[...] = st

    pieces = [(b, c, h) for half in range(2) for b in range(nb) for c in range(half, n_c, 2)
              for h in range(GDN_HEADS)]
    n_first = len(pieces) // 2

    def chunk_rows(x3, b, c, h, lanes):
        return x3[unit_of(b, c, h), (c % 2) * CHUNK:(c % 2 + 1) * CHUNK, lanes]

    wq = jnp.stack([jnp.concatenate(
        [chunk_rows(uwb, b, c, h, slice(hd, 2 * hd)),
         qd[b * ts + c * CHUNK:b * ts + (c + 1) * CHUNK, h * hd:(h + 1) * hd]], axis=0)
        for b, c, h in pieces])
    s_all = jnp.stack([state_in[c][b * GDN_HEADS + h] for b, c, h in pieces])
    rr = _bmm(wq, s_all)
    u_all = jnp.stack([chunk_rows(uw, b, c, h, slice(0, hd)) for b, c, h in pieces])
    v_new = (u_all - rr[:, :CHUNK]).astype(BF16)
    zeros = jnp.zeros_like(v_new[:n_first])
    v_pad = jnp.concatenate([jnp.concatenate([v_new[:n_first], zeros], axis=1),
                             jnp.concatenate([zeros, v_new[n_first:]], axis=1)], axis=0)
    p_rows = jnp.stack([chunk_rows(p, b, c, h, slice(None)) for b, c, h in pieces])
    o_p = rr[:, CHUNK:] + _bmm(p_rows, v_pad)
    where = {bch: i for i, bch in enumerate(pieces)}
    o = jnp.concatenate(
        [jnp.concatenate([o_p[where[(b, c, h)]] for h in range(GDN_HEADS)], axis=1)
         for b in range(nb) for c in range(n_c)], axis=0)
    ms = _dot((o * o).astype(BF16), bd) * (1.0 / hd)
    gate = gate_ref[...].reshape(r, gw)
    o = o * lax.rsqrt(ms + NORM_EPS) * onorm_ref[...] * _silu(gate)
    o_ref[...] = o.astype(BF16).reshape(nb, ts, gw)


def _gdn_call(qkv, gate, ab, convw, alog, dtb, onorm, bd, ltri, batch):
    t = qkv.shape[0]
    s = t // batch
    ts = min(GDN_TS, s)
    gw = GDN_WIDTH
    n = s // ts
    r = batch * ts
    ahead = lambda w: pl.BlockSpec((batch, ts, w), lambda i: (0, jnp.minimum(i, n - 1), 0))
    behind = lambda w: pl.BlockSpec((batch, ts, w), lambda i: (0, jnp.maximum(i - 1, 0), 0))
    out = pl.pallas_call(
        _gdn_kernel,
        grid=(n + 1,),
        in_specs=[ahead(3 * gw), behind(gw), ahead(LANES), _full_spec(convw.shape),
                  _full_spec(alog.shape), _full_spec(dtb.shape), _full_spec(onorm.shape),
                  _full_spec(bd.shape), _full_spec(ltri.shape)],
        out_specs=behind(gw),
        out_shape=jax.ShapeDtypeStruct((batch, s, gw), BF16),
        scratch_shapes=[pltpu.VMEM((batch, ts + SUBLANES, 3 * gw), F32),
                        pltpu.VMEM((batch * GDN_HEADS, GDN_HEAD_DIM, GDN_HEAD_DIM), F32),
                        pltpu.VMEM((2, r, gw), F32), pltpu.VMEM((2, r, gw), F32),
                        pltpu.VMEM((2, r, gw), F32), pltpu.VMEM((2, r, LANES), F32),
                        pltpu.VMEM((2, r, LANES), F32)],
        compiler_params=pltpu.CompilerParams(dimension_semantics=("arbitrary",),
                                             vmem_limit_bytes=VMEM_LIMIT_BYTES),
        name="gated_deltanet",
    )(qkv.reshape(batch, s, 3 * gw), gate.reshape(batch, s, gw), ab.reshape(batch, s, LANES),
      convw, alog, dtb, onorm, bd, ltri)
    return out.reshape(t, gw)


def _att_kernel(q_ref, k_ref, v_ref, o_ref, va_s, vb_s, m_s, acc_s, s_s):
    tq = q_ref.shape[0]
    tk = tq
    qi = pl.program_id(2)
    hp = HEAD_PAD

    @pl.when(qi == 0)
    def _():
        lane = lax.broadcasted_iota(jnp.int32, (1, 2 * hp), 1)
        v = v_ref[...]
        va_s[...] = jnp.where(lane < hp, v, jnp.zeros_like(v))
        vb_s[...] = jnp.where(lane >= hp, v, jnp.zeros_like(v))

    m_s[...] = jnp.full(m_s.shape, -jnp.inf, F32)
    acc_s[...] = jnp.zeros(acc_s.shape, F32)

    def scores(j0, visible):
        kk = pl.ds(j0, tk)
        for hh in range(2):
            hl = slice(hh * hp, (hh + 1) * hp)
            s = _dot_nt(q_ref[:, hl], k_ref[kk, hl])
            s_s[hh] = s if visible is None else jnp.where(visible, s, -jnp.inf)

    def softmax_pv(j0):
        kk = pl.ds(j0, tk)
        alphas, probs = [], []
        for hh in range(2):
            s = s_s[hh]
            m_old = m_s[hh]
            m_new = jnp.maximum(m_old, jnp.max(s, axis=1, keepdims=True))
            m_s[hh] = m_new
            alphas.append(jnp.exp2(m_old - m_new))
            probs.append(jnp.exp2(s - jnp.tile(m_new, (1, tk // LANES))).astype(BF16))
        alpha = jnp.concatenate(alphas, axis=1)
        acc_s[...] = acc_s[...] * alpha + _dot(probs[0], va_s[kk, :]) + _dot(probs[1], vb_s[kk, :])

    ri = lax.broadcasted_iota(jnp.int32, (tq, tk), 0)
    ci = lax.broadcasted_iota(jnp.int32, (tq, tk), 1)
    scores(pl.multiple_of(qi * tq, tq), (ci // CHUNK) <= (ri // CHUNK))

    def step(j):
        cur = jnp.where(j == 0, qi, j - 1)
        softmax_pv(pl.multiple_of(cur * tk, tk))
        scores(pl.multiple_of(j * tk, tk), None)

    done = 0
    for n_peel in (1, 2):
        assert n_peel < ATT_UNROLL
        peel = jnp.bitwise_and(qi, n_peel)

        @pl.when(peel != 0)
        def _(done=done, n_peel=n_peel):
            for u in range(n_peel):
                step(done + u)

        done = done + peel

    def body(i, carry):
        for u in range(ATT_UNROLL):
            step(done + ATT_UNROLL * i + u)
        return carry

    lax.fori_loop(0, lax.div(qi, ATT_UNROLL), body, 0)
    last = jnp.where(qi == 0, 0, qi - 1)
    softmax_pv(pl.multiple_of(last * tk, tk))

    acc = acc_s[...]
    oa = acc[:, 0:hp] / acc[:, MLA_V_DIM:MLA_V_DIM + 1]
    ob = acc[:, hp:2 * hp] / acc[:, hp + MLA_V_DIM:hp + MLA_V_DIM + 1]
    lane = lax.broadcasted_iota(jnp.int32, (1, hp), 1)
    o_ref[...] = jnp.where(lane < MLA_V_DIM, oa, pltpu.roll(ob, MLA_V_DIM, 1)).astype(BF16)


def _att_call(q, k, v, batch):
    t = q.shape[0]
    s = t // batch
    tq = min(ATT_TQ, s)
    nq = s // tq
    n_pairs = MLA_HEADS // 2
    w2 = 2 * HEAD_PAD
    return pl.pallas_call(
        _att_kernel,
        grid=(batch, n_pairs, nq),
        in_specs=[pl.BlockSpec((tq, w2), lambda b, p, i: (b * nq + i, p)),
                  pl.BlockSpec((s, w2), lambda b, p, i: (b, p)),
                  pl.BlockSpec((s, w2), lambda b, p, i: (b, p))],
        out_specs=pl.BlockSpec((tq, LANES), lambda b, p, i: (b * nq + i, p)),
        out_shape=jax.ShapeDtypeStruct((t, MLA_WIDTH), BF16),
        scratch_shapes=[pltpu.VMEM((s, w2), BF16), pltpu.VMEM((s, w2), BF16),
                        pltpu.VMEM((2, tq, LANES), F32), pltpu.VMEM((tq, w2), F32),
                        pltpu.VMEM((2, tq, tq), F32)],
        compiler_params=pltpu.CompilerParams(
            dimension_semantics=("arbitrary", "arbitrary", "arbitrary"),
            vmem_limit_bytes=VMEM_LIMIT_BYTES),
        name="mla_attention",
    )(q, k, v)


def _ffn_kernel(x_ref, og_ref, om_ref, wo_ref, g1_ref, g2_ref, wup_ref, wdn_ref, g3_ref, out_ref):
    mixed = _dot(og_ref[...], wo_ref[0:GDN_WIDTH, :]) + _dot(om_ref[...], wo_ref[GDN_WIDTH:, :])
    x1 = x_ref[...] + _rms(mixed, g1_ref[...])
    h = _rms(x1, g2_ref[...]).astype(BF16)
    n_chunk = D_FF // D_MODEL
    f = None
    for c in range(n_chunk):
        cc = slice(c * D_MODEL, (c + 1) * D_MODEL)
        a = jnp.maximum(_dot(h, wup_ref[:, cc]), 0.0)
        part = _dot((a * a).astype(BF16), wdn_ref[cc, :])
        f = part if f is None else f + part
    out_ref[...] = x1 + _rms(f, g3_ref[...])


def _ffn_call(x2, og, om, wo, g1, g2, wup, wdn, g3):
    t = x2.shape[0]
    tm = min(FFN_TM, t)
    row = lambda w: pl.BlockSpec((tm, w), lambda i: (i, 0))
    const = lambda a: pl.BlockSpec(a.shape, lambda i: (0,) * a.ndim, pipeline_mode=pl.Buffered(1))
    return pl.pallas_call(
        _ffn_kernel,
        grid=(t // tm,),
        in_specs=[row(D_MODEL), row(GDN_WIDTH), row(MLA_WIDTH), const(wo), const(g1), const(g2),
                  const(wup), const(wdn), const(g3)],
        out_specs=row(D_MODEL),
        out_shape=jax.ShapeDtypeStruct((t, D_MODEL), F32),
        compiler_params=pltpu.CompilerParams(dimension_semantics=("arbitrary",),
                                             vmem_limit_bytes=VMEM_LIMIT_BYTES),
        name="out_proj_mlp",
    )(x2, og, om, wo, g1, g2, wup, wdn, g3)


def _rot_half(w):
    half = MLA_ROPE_DIM // 2
    return jnp.concatenate([-w[..., half:], w[..., :half]], axis=-1)


def _layer_weights(w_in, w_q_up, w_kv_up):
    d = w_in.shape[0]
    gw = GDN_WIDTH
    o_a = 4 * gw
    o_b = o_a + GDN_HEADS
    o_cq = o_b + GDN_HEADS
    o_ckv = o_cq + MLA_Q_RANK
    o_kr = o_ckv + MLA_KV_RANK
    zeros = lambda n: jnp.zeros((d, n), w_in.dtype)
    kr = w_in[:, o_kr:o_kr + MLA_ROPE_DIM]
    w1 = jnp.concatenate([
        w_in[:, :o_a],
        w_in[:, o_a:o_b], w_in[:, o_b:o_cq], zeros(LANES - 2 * GDN_HEADS),
        w_in[:, o_cq:o_ckv], w_in[:, o_ckv:o_kr],
        zeros(ROPE_LANE0), kr, _rot_half(kr)], axis=1).astype(BF16)

    dqk = MLA_NOPE_DIM + MLA_ROPE_DIM
    wq3 = w_q_up.reshape(MLA_Q_RANK, MLA_HEADS, dqk)
    nope, rope = wq3[..., :MLA_NOPE_DIM], wq3[..., MLA_NOPE_DIM:]
    pad = jnp.zeros((MLA_Q_RANK, MLA_HEADS, HEAD_PAD - dqk), w_q_up.dtype)
    wq_a = jnp.concatenate([nope, rope, pad], axis=-1).reshape(MLA_Q_RANK, MLA_HEADS * HEAD_PAD)
    wq_b = jnp.concatenate([jnp.zeros_like(nope), _rot_half(rope), pad], axis=-1)
    wq = jnp.concatenate([wq_a, wq_b.reshape(MLA_Q_RANK, MLA_HEADS * HEAD_PAD)], axis=1).astype(BF16)

    wkv3 = w_kv_up.reshape(MLA_KV_RANK, MLA_HEADS, MLA_NOPE_DIM + MLA_V_DIM)
    k_nope, v = wkv3[..., :MLA_NOPE_DIM], wkv3[..., MLA_NOPE_DIM:]
    padk = jnp.zeros((MLA_KV_RANK, MLA_HEADS, HEAD_PAD - MLA_NOPE_DIM), w_kv_up.dtype)
    wk = jnp.concatenate([k_nope, padk], axis=-1).reshape(MLA_KV_RANK, MLA_HEADS * HEAD_PAD)
    padv = jnp.zeros((MLA_KV_RANK, MLA_HEADS, HEAD_PAD - MLA_V_DIM), w_kv_up.dtype)
    wv = jnp.concatenate([v, padv], axis=-1).reshape(MLA_KV_RANK, MLA_HEADS * HEAD_PAD)
    return w1, wq, wk.astype(BF16), wv.astype(BF16)


def _constants(ts):
    lane = np.arange(LANES)
    src = np.where(lane >= ROPE_LANE1, lane - MLA_ROPE_DIM, lane)
    tgt = np.arange(MLA_HEADS * HEAD_PAD)
    e_mat = (src[:, None] == (tgt % HEAD_PAD)[None, :]) & (lane[:, None] >= ROPE_LANE0)
    vone = ((tgt % HEAD_PAD) == MLA_V_DIM)[None, :]
    gl = np.arange(GDN_WIDTH) // GDN_HEAD_DIM
    bd = gl[:, None] == gl[None, :]
    r = np.arange(ts)
    ltri = ((r[:, None] // CHUNK) == (r[None, :] // CHUNK)) & (r[:, None] >= r[None, :])
    as_bf16 = lambda m: jnp.asarray(m.astype(np.float32), dtype=BF16)
    return as_bf16(e_mat), jnp.asarray(vone.astype(np.float32)), as_bf16(bd), as_bf16(ltri)


def _lane_row(vec, width=LANES):
    return jnp.zeros((1, width), F32).at[0, :vec.shape[0]].set(vec.astype(F32))


def kernel(x, positions, mix_pre_norm, w_in, conv_w, a_log, dt_bias, gdn_out_norm, q_norm, w_q_up,
           kv_norm, w_kv_up, w_out, mix_post_norm, ffn_pre_norm, w_up, w_down, ffn_post_norm):
    batch, seq, d = x.shape
    depth = w_in.shape[0]
    t = batch * seq
    x2 = x.reshape(t, d)
    pos_col = positions.reshape(t, 1)

    inv_freq = ROPE_THETA ** (-jnp.arange(0, MLA_ROPE_DIM, 2, dtype=F32) / MLA_ROPE_DIM)
    lane = jnp.arange(LANES)
    freq_row = jnp.where(lane >= ROPE_LANE0, inv_freq[lane % (MLA_ROPE_DIM // 2)], 0.0)[None, :]
    cos_t, sin_t = _rope_tables(pos_col, freq_row.astype(F32))

    e_mat, vone, bd, ltri = _constants(batch * min(GDN_TS, seq))
    row = lambda v: v.astype(F32)[None, :]
    for l in range(depth):
        w1, wq, wk, wv = _layer_weights(w_in[l], w_q_up[l], w_kv_up[l])
        qkv, gate, ab, q, k, v = _proj_call(
            x2, cos_t, sin_t, row(mix_pre_norm[l]), w1, row(q_norm[l]), wq, row(kv_norm[l]), wk, wv,
            e_mat, vone)
        o_gdn = _gdn_call(qkv, gate, ab, conv_w[l].astype(F32), _lane_row(a_log[l]),
                          _lane_row(dt_bias[l]), row(jnp.tile(gdn_out_norm[l], GDN_HEADS)), bd, ltri,
                          batch)
        o_mla = _att_call(q, k, v, batch)
        x2 = _ffn_call(x2, o_gdn, o_mla, w_out[l].astype(BF16), row(mix_post_norm[l]),
                       row(ffn_pre_norm[l]), w_up[l].astype(BF16), w_down[l].astype(BF16),
                       row(ffn_post_norm[l]))
    return x2.reshape(batch, seq, d)
```

```python
import jax
import jax.numpy as jnp
import numpy as np
from jax import lax
from jax.experimental import pallas as pl
from jax.experimental.pallas import tpu as pltpu

F32 = jnp.float32
BF16 = jnp.bfloat16

D_MODEL = 1024
CHUNK = 64
NORM_EPS = 1e-6
GDN_HEADS = 4
GDN_HEAD_DIM = 128
GDN_WIDTH = GDN_HEADS * GDN_HEAD_DIM
CONV_WIDTH = 4
MLA_HEADS = 8
MLA_NOPE_DIM = 64
MLA_ROPE_DIM = 32
MLA_V_DIM = 64
MLA_Q_RANK = 256
MLA_KV_RANK = 128
MLA_WIDTH = MLA_HEADS * MLA_V_DIM
ROPE_THETA = 10000.0
D_FF = 4 * D_MODEL
LOG2_E = 1.4426950408889634

LANES = 128
SUBLANES = 8
VMEM_LIMIT_BYTES = 56 * 1024 * 1024

COL_QKV = 0
COL_GATE = 3 * GDN_WIDTH
COL_AB = COL_GATE + GDN_WIDTH
COL_CQ = COL_AB + LANES
COL_CKV = COL_CQ + MLA_Q_RANK
COL_KR = COL_CKV + MLA_KV_RANK
N_PROJ = COL_KR + LANES
HEAD_PAD = LANES
ROPE_LANE0 = MLA_NOPE_DIM
ROPE_LANE1 = MLA_NOPE_DIM + MLA_ROPE_DIM

PROJ_TM = 512
FFN_TM = 1024
GDN_TS = 256
PAIR = 2 * CHUNK
ATT_TQ = 512
ATT_UNROLL = 4
ROPE_TM = 2048


def _dot(a, b):
    return jnp.dot(a, b, preferred_element_type=F32)


def _dot_nt(a, b):
    return lax.dot_general(a, b, (((1,), (1,)), ((), ())), preferred_element_type=F32)


def _dot_tn(a, b):
    return lax.dot_general(a, b, (((0,), (0,)), ((), ())), preferred_element_type=F32)


def _rms(x, gain):
    return x * lax.rsqrt(jnp.mean(x * x, axis=-1, keepdims=True) + NORM_EPS) * gain


def _silu(x):
    return x * jax.nn.sigmoid(x)


def _full_spec(shape):
    return pl.BlockSpec(shape, lambda *_: (0,) * len(shape))


def _rope_kernel(pos_ref, freq_ref, cos_ref, sin_ref):
    ang = pos_ref[...].astype(F32) * freq_ref[...]
    cos_ref[...] = jnp.cos(ang)
    sin_ref[...] = jnp.sin(ang)


def _rope_tables(pos_col, freq_row):
    t = pos_col.shape[0]
    tm = min(ROPE_TM, t)
    return pl.pallas_call(
        _rope_kernel,
        grid=(t // tm,),
        in_specs=[pl.BlockSpec((tm, 1), lambda i: (i, 0)), _full_spec((1, LANES))],
        out_specs=[pl.BlockSpec((tm, LANES), lambda i: (i, 0))] * 2,
        out_shape=[jax.ShapeDtypeStruct((t, LANES), F32)] * 2,
        name="rope_tables",
    )(pos_col, freq_row)


def _proj_kernel(x_ref, cos_ref, sin_ref, g_ref, w1_ref, qn_ref, wq_ref, kvn_ref, wk_ref, wv_ref,
                 vone_ref, qkv_ref, gate_ref, ab_ref, q_ref, k_ref, v_ref):
    cos = cos_ref[...]
    sin = sin_ref[...]
    h = _rms(x_ref[...], g_ref[...]).astype(BF16)
    qkv_ref[...] = _dot(h, w1_ref[:, COL_QKV:COL_GATE])
    gate_ref[...] = _dot(h, w1_ref[:, COL_GATE:COL_AB])
    small = _dot(h, w1_ref[:, COL_AB:N_PROJ])
    ab_ref[...] = small[:, 0:COL_CQ - COL_AB]
    cq = small[:, COL_CQ - COL_AB:COL_CKV - COL_AB]
    ckv = small[:, COL_CKV - COL_AB:COL_KR - COL_AB]
    ykr = small[:, COL_KR - COL_AB:N_PROJ - COL_AB]

    lane = lax.broadcasted_iota(jnp.int32, (1, LANES), 1)
    in_rope = (lane >= ROPE_LANE0) & (lane < ROPE_LANE1)
    scale = (MLA_NOPE_DIM + MLA_ROPE_DIM) ** -0.5 * LOG2_E
    cq_pat = jnp.where(lane < ROPE_LANE1, cos, 0.0) * scale
    sq_pat = jnp.where(in_rope, sin, 0.0) * scale
    n_rep = MLA_HEADS
    yq = _dot(_rms(cq, qn_ref[...]).astype(BF16), wq_ref[...])
    half = MLA_HEADS * HEAD_PAD
    q = yq[:, :half] * jnp.tile(cq_pat, (1, n_rep)) + yq[:, half:] * jnp.tile(sq_pat, (1, n_rep))
    q_ref[...] = q.astype(BF16)

    ckvn = _rms(ckv, kvn_ref[...]).astype(BF16)
    krf = ykr * jnp.where(lane < ROPE_LANE1, cos, sin)
    k_rope = jnp.where(in_rope, krf + pltpu.roll(krf, LANES - MLA_ROPE_DIM, 1), 0.0)
    k_ref[...] = (_dot(ckvn, wk_ref[...]) + jnp.tile(k_rope, (1, n_rep))).astype(BF16)
    v_ref[...] = (_dot(ckvn, wv_ref[...]) + vone_ref[...]).astype(BF16)


def _proj_call(x2, cos_t, sin_t, g, w1, qn, wq, kvn, wk, wv, vone):
    t = x2.shape[0]
    tm = min(PROJ_TM, t)
    row = lambda w: pl.BlockSpec((tm, w), lambda i: (i, 0))
    hp = MLA_HEADS * HEAD_PAD
    return pl.pallas_call(
        _proj_kernel,
        grid=(t // tm,),
        in_specs=[row(D_MODEL), row(LANES), row(LANES), _full_spec(g.shape), _full_spec(w1.shape),
                  _full_spec(qn.shape), _full_spec(wq.shape), _full_spec(kvn.shape),
                  _full_spec(wk.shape), _full_spec(wv.shape), _full_spec(vone.shape)],
        out_specs=[row(3 * GDN_WIDTH), row(GDN_WIDTH), row(LANES), row(hp), row(hp), row(hp)],
        out_shape=[jax.ShapeDtypeStruct((t, 3 * GDN_WIDTH), F32),
                   jax.ShapeDtypeStruct((t, GDN_WIDTH), F32),
                   jax.ShapeDtypeStruct((t, LANES), F32),
                   jax.ShapeDtypeStruct((t, hp), BF16),
                   jax.ShapeDtypeStruct((t, hp), BF16),
                   jax.ShapeDtypeStruct((t, hp), BF16)],
        compiler_params=pltpu.CompilerParams(dimension_semantics=("arbitrary",),
                                             vmem_limit_bytes=VMEM_LIMIT_BYTES),
        name="in_proj",
    )(x2, cos_t, sin_t, g, w1, qn, wq, kvn, wk, wv, vone)


def _bmm(a, b):
    return jnp.einsum("uij,ujk->uik", a.astype(BF16), b.astype(BF16), preferred_element_type=F32)


def _bmm_nt(a, b):
    return jnp.einsum("uik,ujk->uij", a.astype(BF16), b.astype(BF16), preferred_element_type=F32)


def _unit_lower_inverse_minus_eye(a, m16, m32, m64):
    a0 = jnp.where(m16, a, 0.0)
    a2 = _bmm(a0, a0)
    a4 = _bmm(a2, a2)
    a8 = _bmm(a4, a4)
    xs = -a0
    for p in (a2, a4, a8):
        xs = xs + p + _bmm(xs, p)
    for m in (m32, m64):
        off = jnp.where(m, a, 0.0)
        y = off + _bmm(xs, off)
        xs = xs - (y + _bmm(y, xs))
    return xs


def _gdn_kernel(qkv_ref, gate_ref, ab_ref, convw_ref, alog_ref, dtb_ref, onorm_ref, bd_ref, ltri_ref,
                o_ref, xe_s, state_s, qn_s, kn_s, v_s, g_s, beta_s):
    nb, ts = qkv_ref.shape[0], qkv_ref.shape[1]
    r = nb * ts
    n_pair = r // PAIR
    pairs_per_batch = ts // PAIR
    hd = GDN_HEAD_DIM
    gw = GDN_WIDTH
    step = pl.program_id(0)

    @pl.when(step == 0)
    def _():
        xe_s[:, 0:SUBLANES, :] = jnp.zeros((nb, SUBLANES, 3 * gw), F32)
        state_s[...] = jnp.zeros_like(state_s)
        for ref in (qn_s, kn_s, v_s, g_s, beta_s):
            ref[...] = jnp.zeros_like(ref)

    wr = jnp.bitwise_and(step, 1)
    rd = 1 - wr
    qn = qn_s[rd]
    kn = kn_s[rd]
    v = v_s[rd]
    g = g_s[rd]
    beta = beta_s[rd]
    bd = bd_ref[...]
    gt = g.T

    def per_head_lanes(x, lane0):
        return jnp.concatenate(
            [jnp.broadcast_to(x[:, lane0 + h:lane0 + h + 1], (r, hd)) for h in range(GDN_HEADS)], axis=1)

    g_b = per_head_lanes(g, 0)
    beta_b = per_head_lanes(beta, GDN_HEADS)
    n_chunk = r // CHUNK
    gl_b = jnp.broadcast_to(g_b.reshape(n_chunk, CHUNK, gw)[:, CHUNK - 1:CHUNK, :],
                            (n_chunk, CHUNK, gw)).reshape(r, gw)
    eg = jnp.exp2(g_b)
    kb = kn * beta_b
    rhs_v = v * beta_b
    rhs_k = kb * eg
    qd = (qn * eg).astype(BF16)
    kd = kn * jnp.exp2(gl_b - g_b)
    dec_b = jnp.exp2(gl_b)

    ri = lax.broadcasted_iota(jnp.int32, (PAIR, PAIR), 0)
    ci = lax.broadcasted_iota(jnp.int32, (PAIR, PAIR), 1)
    same = (ri // CHUNK) == (ci // CHUNK)
    causal = same & (ri >= ci)
    strict = same & (ri > ci)
    m16 = strict & ((ri // 16) == (ci // 16))
    m32 = strict & ((ri // 32) == (ci // 32)) & ((ri // 16) != (ci // 16))
    m64 = strict & ((ri // 32) != (ci // 32))
    first_lanes = ci < CHUNK

    units = [(j, h) for j in range(n_pair) for h in range(GDN_HEADS)]

    def tile(x, j, h):
        return x[j * PAIR:(j + 1) * PAIR, h * hd:(h + 1) * hd]

    def stack(x):
        return jnp.stack([tile(x, j, h) for j, h in units])

    k_u = stack(kn)
    g_rows = jnp.stack([gt[h:h + 1, j * PAIR:(j + 1) * PAIR] for j, h in units])
    dc = jnp.exp2(jnp.where(causal, stack(g_b) - g_rows, -jnp.inf))
    a = _bmm_nt(stack(kb), k_u) * jnp.where(strict, dc, 0.0)
    p = (_bmm_nt(stack(qn), k_u) * dc).astype(BF16)
    xs = _unit_lower_inverse_minus_eye(a, m16, m32, m64)
    rhs = jnp.concatenate([stack(rhs_v), stack(rhs_k)], axis=2)
    uw = rhs + _bmm(xs, rhs)

    kdt = jnp.stack([tile(kd, j, h).T for j, h in units])
    uwb = uw.astype(BF16)
    nbm = [_bmm(jnp.where(first_lanes, kdt, 0.0), uwb),
           _bmm(jnp.where(first_lanes, 0.0, kdt), uwb)]

    xe_s[:, SUBLANES:, :] = qkv_ref[...]
    y = convw_ref[CONV_WIDTH - 1:CONV_WIDTH, :] * xe_s[:, SUBLANES:SUBLANES + ts, :]
    for i in range(1, CONV_WIDTH):
        y = y + (convw_ref[CONV_WIDTH - 1 - i:CONV_WIDTH - i, :]
                 * xe_s[:, SUBLANES - i:SUBLANES - i + ts, :])
    xe_s[:, 0:SUBLANES, :] = qkv_ref[:, ts - SUBLANES:ts, :]
    y = _silu(y).reshape(r, 3 * gw)
    q_in = y[:, 0:gw]
    k_in = y[:, gw:2 * gw]
    qn_s[wr] = q_in * lax.rsqrt(_dot((q_in * q_in).astype(BF16), bd) + NORM_EPS) * (hd ** -0.5)
    kn_s[wr] = k_in * lax.rsqrt(_dot((k_in * k_in).astype(BF16), bd) + NORM_EPS)
    v_s[wr] = y[:, 2 * gw:3 * gw]

    ab = ab_ref[...].reshape(r, LANES)
    beta_s[wr] = jax.nn.sigmoid(ab)
    z = ab + dtb_ref[...]
    softplus = jnp.maximum(z, 0.0) + jnp.log(1.0 + jnp.exp(-jnp.abs(z)))
    la = (-LOG2_E) * jnp.exp(alog_ref[...]) * softplus
    la_hi = la.astype(BF16)
    r1 = la - la_hi.astype(F32)
    la_mid = r1.astype(BF16)
    la_lo = (r1 - la_mid.astype(F32)).astype(BF16)
    ltri = ltri_ref[...]
    g_s[wr] = _dot(ltri, la_hi) + _dot(ltri, la_mid) + _dot(ltri, la_lo)

    n_c = ts // CHUNK
    chains = [(b, h) for b in range(nb) for h in range(GDN_HEADS)]

    def unit_of(b, c, h):
        return (b * pairs_per_batch + c // 2) * GDN_HEADS + h

    st = state_s[...]
    state_in = []
    for c in range(n_c):
        nb_c = jnp.stack([nbm[c % 2][unit_of(b, c, h)] for b, h in chains])
        dec = jnp.stack([jnp.tile(dec_b[b * ts + c * CHUNK:b * ts + (c + 1) * CHUNK, h * hd:(h + 1) * hd],
                                  (2, 1)) for b, h in chains])
        sb = st.astype(BF16)
        state_in.append(sb)
        st = st * dec - _bmm(nb_c[:, :, hd:], sb) + nb_c[:, :, :hd]
    state_s[...] = st

    pieces = [(b, c, h) for half in range(2) for b in range(nb) for c in range(half, n_c, 2)
              for h in range(GDN_HEADS)]
    n_first = len(pieces) // 2

    def chunk_rows(x3, b, c, h, lanes):
        return x3[unit_of(b, c, h), (c % 2) * CHUNK:(c % 2 + 1) * CHUNK, lanes]

    wq = jnp.stack([jnp.concatenate(
        [chunk_rows(uwb, b, c, h, slice(hd, 2 * hd)),
         qd[b * ts + c * CHUNK:b * ts + (c + 1) * CHUNK, h * hd:(h + 1) * hd]], axis=0)
        for b, c, h in pieces])
    s_all = jnp.stack([state_in[c][b * GDN_HEADS + h] for b, c, h in pieces])
    rr = _bmm(wq, s_all)
    u_all = jnp.stack([chunk_rows(uw, b, c, h, slice(0, hd)) for b, c, h in pieces])
    v_new = (u_all - rr[:, :CHUNK]).astype(BF16)
    zeros = jnp.zeros_like(v_new[:n_first])
    v_pad = jnp.concatenate([jnp.concatenate([v_new[:n_first], zeros], axis=1),
                             jnp.concatenate([zeros, v_new[n_first:]], axis=1)], axis=0)
    p_rows = jnp.stack([chunk_rows(p, b, c, h, slice(None)) for b, c, h in pieces])
    o_p = rr[:, CHUNK:] + _bmm(p_rows, v_pad)
    where = {bch: i for i, bch in enumerate(pieces)}
    o = jnp.concatenate(
        [jnp.concatenate([o_p[where[(b, c, h)]] for h in range(GDN_HEADS)], axis=1)
         for b in range(nb) for c in range(n_c)], axis=0)
    ms = _dot((o * o).astype(BF16), bd) * (1.0 / hd)
    gate = gate_ref[...].reshape(r, gw)
    o = o * lax.rsqrt(ms + NORM_EPS) * onorm_ref[...] * _silu(gate)
    o_ref[...] = o.astype(BF16).reshape(nb, ts, gw)


def _gdn_call(qkv, gate, ab, convw, alog, dtb, onorm, bd, ltri, batch):
    t = qkv.shape[0]
    s = t // batch
    ts = min(GDN_TS, s)
    gw = GDN_WIDTH
    n = s // ts
    r = batch * ts
    ahead = lambda w: pl.BlockSpec((batch, ts, w), lambda i: (0, jnp.minimum(i, n - 1), 0))
    behind = lambda w: pl.BlockSpec((batch, ts, w), lambda i: (0, jnp.maximum(i - 1, 0), 0))
    out = pl.pallas_call(
        _gdn_kernel,
        grid=(n + 1,),
        in_specs=[ahead(3 * gw), behind(gw), ahead(LANES), _full_spec(convw.shape),
                  _full_spec(alog.shape), _full_spec(dtb.shape), _full_spec(onorm.shape),
                  _full_spec(bd.shape), _full_spec(ltri.shape)],
        out_specs=behind(gw),
        out_shape=jax.ShapeDtypeStruct((batch, s, gw), BF16),
        scratch_shapes=[pltpu.VMEM((batch, ts + SUBLANES, 3 * gw), F32),
                        pltpu.VMEM((batch * GDN_HEADS, GDN_HEAD_DIM, GDN_HEAD_DIM), F32),
                        pltpu.VMEM((2, r, gw), F32), pltpu.VMEM((2, r, gw), F32),
                        pltpu.VMEM((2, r, gw), F32), pltpu.VMEM((2, r, LANES), F32),
                        pltpu.VMEM((2, r, LANES), F32)],
        compiler_params=pltpu.CompilerParams(dimension_semantics=("arbitrary",),
                                             vmem_limit_bytes=VMEM_LIMIT_BYTES),
        name="gated_deltanet",
    )(qkv.reshape(batch, s, 3 * gw), gate.reshape(batch, s, gw), ab.reshape(batch, s, LANES),
      convw, alog, dtb, onorm, bd, ltri)
    return out.reshape(t, gw)


def _att_kernel(q_ref, k_ref, v_ref, o_ref, va_s, vb_s, m_s, acc_s, s_s):
    tq = q_ref.shape[0]
    tk = tq
    qi = pl.program_id(2)
    hp = HEAD_PAD

    @pl.when(qi == 0)
    def _():
        lane = lax.broadcasted_iota(jnp.int32, (1, 2 * hp), 1)
        v = v_ref[...]
        va_s[...] = jnp.where(lane < hp, v, jnp.zeros_like(v))
        vb_s[...] = jnp.where(lane >= hp, v, jnp.zeros_like(v))

    m_s[...] = jnp.full(m_s.shape, -jnp.inf, F32)
    acc_s[...] = jnp.zeros(acc_s.shape, F32)

    def scores(j0, visible):
        kk = pl.ds(j0, tk)
        for hh in range(2):
            hl = slice(hh * hp, (hh + 1) * hp)
            s = _dot_nt(q_ref[:, hl], k_ref[kk, hl])
            s_s[hh] = s if visible is None else jnp.where(visible, s, -jnp.inf)

    def softmax_pv(j0):
        kk = pl.ds(j0, tk)
        alphas, probs = [], []
        for hh in range(2):
            s = s_s[hh]
            m_old = m_s[hh]
            m_new = jnp.maximum(m_old, jnp.max(s, axis=1, keepdims=True))
            m_s[hh] = m_new
            alphas.append(jnp.exp2(m_old - m_new))
            probs.append(jnp.exp2(s - jnp.tile(m_new, (1, tk // LANES))).astype(BF16))
        alpha = jnp.concatenate(alphas, axis=1)
        acc_s[...] = acc_s[...] * alpha + _dot(probs[0], va_s[kk, :]) + _dot(probs[1], vb_s[kk, :])

    ri = lax.broadcasted_iota(jnp.int32, (tq, tk), 0)
    ci = lax.broadcasted_iota(jnp.int32, (tq, tk), 1)
    scores(pl.multiple_of(qi * tq, tq), (ci // CHUNK) <= (ri // CHUNK))

    def step(j):
        cur = jnp.where(j == 0, qi, j - 1)
        softmax_pv(pl.multiple_of(cur * tk, tk))
        scores(pl.multiple_of(j * tk, tk), None)

    done = 0
    for n_peel in [1 << b for b in range(ATT_UNROLL.bit_length() - 1)]:
        peel = jnp.bitwise_and(qi, n_peel)

        @pl.when(peel != 0)
        def _(done=done, n_peel=n_peel):
            for u in range(n_peel):
                step(done + u)

        done = done + peel

    def body(i, carry):
        for u in range(ATT_UNROLL):
            step(done + ATT_UNROLL * i + u)
        return carry

    lax.fori_loop(0, lax.div(qi, ATT_UNROLL), body, 0)
    last = jnp.where(qi == 0, 0, qi - 1)
    softmax_pv(pl.multiple_of(last * tk, tk))

    acc = acc_s[...]
    oa = acc[:, 0:hp] / acc[:, MLA_V_DIM:MLA_V_DIM + 1]
    ob = acc[:, hp:2 * hp] / acc[:, hp + MLA_V_DIM:hp + MLA_V_DIM + 1]
    lane = lax.broadcasted_iota(jnp.int32, (1, hp), 1)
    o_ref[...] = jnp.where(lane < MLA_V_DIM, oa, pltpu.roll(ob, MLA_V_DIM, 1)).astype(BF16)


def _att_call(q, k, v, batch):
    t = q.shape[0]
    s = t // batch
    tq = min(ATT_TQ, s)
    nq = s // tq
    n_pairs = MLA_HEADS // 2
    w2 = 2 * HEAD_PAD
    return pl.pallas_call(
        _att_kernel,
        grid=(batch, n_pairs, nq),
        in_specs=[pl.BlockSpec((tq, w2), lambda b, p, i: (b * nq + i, p)),
                  pl.BlockSpec((s, w2), lambda b, p, i: (b, p)),
                  pl.BlockSpec((s, w2), lambda b, p, i: (b, p))],
        out_specs=pl.BlockSpec((tq, LANES), lambda b, p, i: (b * nq + i, p)),
        out_shape=jax.ShapeDtypeStruct((t, MLA_WIDTH), BF16),
        scratch_shapes=[pltpu.VMEM((s, w2), BF16), pltpu.VMEM((s, w2), BF16),
                        pltpu.VMEM((2, tq, LANES), F32), pltpu.VMEM((tq, w2), F32),
                        pltpu.VMEM((2, tq, tq), F32)],
        compiler_params=pltpu.CompilerParams(
            dimension_semantics=("arbitrary", "arbitrary", "arbitrary"),
            vmem_limit_bytes=VMEM_LIMIT_BYTES),
        name="mla_attention",
    )(q, k, v)


def _ffn_kernel(x_ref, og_ref, om_ref, wo_ref, g1_ref, g2_ref, wup_ref, wdn_ref, g3_ref, out_ref):
    mixed = _dot(og_ref[...], wo_ref[0:GDN_WIDTH, :]) + _dot(om_ref[...], wo_ref[GDN_WIDTH:, :])
    x1 = x_ref[...] + _rms(mixed, g1_ref[...])
    h = _rms(x1, g2_ref[...]).astype(BF16)
    n_chunk = D_FF // D_MODEL
    f = None
    for c in range(n_chunk):
        cc = slice(c * D_MODEL, (c + 1) * D_MODEL)
        a = jnp.maximum(_dot(h, wup_ref[:, cc]), 0.0)
        part = _dot((a * a).astype(BF16), wdn_ref[cc, :])
        f = part if f is None else f + part
    out_ref[...] = x1 + _rms(f, g3_ref[...])


def _ffn_call(x2, og, om, wo, g1, g2, wup, wdn, g3):
    t = x2.shape[0]
    tm = min(FFN_TM, t)
    row = lambda w: pl.BlockSpec((tm, w), lambda i: (i, 0))
    const = lambda a: pl.BlockSpec(a.shape, lambda i: (0,) * a.ndim, pipeline_mode=pl.Buffered(1))
    return pl.pallas_call(
        _ffn_kernel,
        grid=(t // tm,),
        in_specs=[row(D_MODEL), row(GDN_WIDTH), row(MLA_WIDTH), const(wo), const(g1), const(g2),
                  const(wup), const(wdn), const(g3)],
        out_specs=row(D_MODEL),
        out_shape=jax.ShapeDtypeStruct((t, D_MODEL), F32),
        compiler_params=pltpu.CompilerParams(dimension_semantics=("arbitrary",),
                                             vmem_limit_bytes=VMEM_LIMIT_BYTES),
        name="out_proj_mlp",
    )(x2, og, om, wo, g1, g2, wup, wdn, g3)


def _rot_half(w):
    half = MLA_ROPE_DIM // 2
    return jnp.concatenate([-w[..., half:], w[..., :half]], axis=-1)


def _layer_weights(w_in, w_q_up, w_kv_up):
    d = w_in.shape[0]
    gw = GDN_WIDTH
    o_a = 4 * gw
    o_b = o_a + GDN_HEADS
    o_cq = o_b + GDN_HEADS
    o_ckv = o_cq + MLA_Q_RANK
    o_kr = o_ckv + MLA_KV_RANK
    zeros = lambda n: jnp.zeros((d, n), w_in.dtype)
    kr = w_in[:, o_kr:o_kr + MLA_ROPE_DIM]
    w1 = jnp.concatenate([
        w_in[:, :o_a],
        w_in[:, o_a:o_b], w_in[:, o_b:o_cq], zeros(LANES - 2 * GDN_HEADS),
        w_in[:, o_cq:o_ckv], w_in[:, o_ckv:o_kr],
        zeros(ROPE_LANE0), kr, _rot_half(kr)], axis=1).astype(BF16)

    dqk = MLA_NOPE_DIM + MLA_ROPE_DIM
    wq3 = w_q_up.reshape(MLA_Q_RANK, MLA_HEADS, dqk)
    nope, rope = wq3[..., :MLA_NOPE_DIM], wq3[..., MLA_NOPE_DIM:]
    pad = jnp.zeros((MLA_Q_RANK, MLA_HEADS, HEAD_PAD - dqk), w_q_up.dtype)
    wq_a = jnp.concatenate([nope, rope, pad], axis=-1).reshape(MLA_Q_RANK, MLA_HEADS * HEAD_PAD)
    wq_b = jnp.concatenate([jnp.zeros_like(nope), _rot_half(rope), pad], axis=-1)
    wq = jnp.concatenate([wq_a, wq_b.reshape(MLA_Q_RANK, MLA_HEADS * HEAD_PAD)], axis=1).astype(BF16)

    wkv3 = w_kv_up.reshape(MLA_KV_RANK, MLA_HEADS, MLA_NOPE_DIM + MLA_V_DIM)
    k_nope, v = wkv3[..., :MLA_NOPE_DIM], wkv3[..., MLA_NOPE_DIM:]
    padk = jnp.zeros((MLA_KV_RANK, MLA_HEADS, HEAD_PAD - MLA_NOPE_DIM), w_kv_up.dtype)
    wk = jnp.concatenate([k_nope, padk], axis=-1).reshape(MLA_KV_RANK, MLA_HEADS * HEAD_PAD)
    padv = jnp.zeros((MLA_KV_RANK, MLA_HEADS, HEAD_PAD - MLA_V_DIM), w_kv_up.dtype)
    wv = jnp.concatenate([v, padv], axis=-1).reshape(MLA_KV_RANK, MLA_HEADS * HEAD_PAD)
    return w1, wq, wk.astype(BF16), wv.astype(BF16)


def _constants(ts):
    tgt = np.arange(MLA_HEADS * HEAD_PAD)
    vone = ((tgt % HEAD_PAD) == MLA_V_DIM)[None, :]
    gl = np.arange(GDN_WIDTH) // GDN_HEAD_DIM
    bd = gl[:, None] == gl[None, :]
    r = np.arange(ts)
    ltri = ((r[:, None] // CHUNK) == (r[None, :] // CHUNK)) & (r[:, None] >= r[None, :])
    as_bf16 = lambda m: jnp.asarray(m.astype(np.float32), dtype=BF16)
    return jnp.asarray(vone.astype(np.float32)), as_bf16(bd), as_bf16(ltri)


def _lane_row(vec, width=LANES):
    return jnp.zeros((1, width), F32).at[0, :vec.shape[0]].set(vec.astype(F32))


def kernel(x, positions, mix_pre_norm, w_in, conv_w, a_log, dt_bias, gdn_out_norm, q_norm, w_q_up,
           kv_norm, w_kv_up, w_out, mix_post_norm, ffn_pre_norm, w_up, w_down, ffn_post_norm):
    batch, seq, d = x.shape
    depth = w_in.shape[0]
    t = batch * seq
    x2 = x.reshape(t, d)
    pos_col = positions.reshape(t, 1)

    inv_freq = ROPE_THETA ** (-jnp.arange(0, MLA_ROPE_DIM, 2, dtype=F32) / MLA_ROPE_DIM)
    lane = jnp.arange(LANES)
    freq_row = jnp.where(lane >= ROPE_LANE0, inv_freq[lane % (MLA_ROPE_DIM // 2)], 0.0)[None, :]
    cos_t, sin_t = _rope_tables(pos_col, freq_row.astype(F32))

    vone, bd, ltri = _constants(batch * min(GDN_TS, seq))
    row = lambda v: v.astype(F32)[None, :]
    for l in range(depth):
        w1, wq, wk, wv = _layer_weights(w_in[l], w_q_up[l], w_kv_up[l])
        qkv, gate, ab, q, k, v = _proj_call(
            x2, cos_t, sin_t, row(mix_pre_norm[l]), w1, row(q_norm[l]), wq, row(kv_norm[l]), wk, wv,
            vone)
        o_gdn = _gdn_call(qkv, gate, ab, conv_w[l].astype(F32), _lane_row(a_log[l]),
                          _lane_row(dt_bias[l]), row(jnp.tile(gdn_out_norm[l], GDN_HEADS)), bd, ltri,
                          batch)
        o_mla = _att_call(q, k, v, batch)
        x2 = _ffn_call(x2, o_gdn, o_mla, w_out[l].astype(BF16), row(mix_post_norm[l]),
                       row(ffn_pre_norm[l]), w_up[l].astype(BF16), w_down[l].astype(BF16),
                       row(ffn_post_norm[l]))
    return x2.reshape(batch, seq, d)
```
